```python
import jax, jax.numpy as jnp
from jax import lax
import numpy as np

D_MODEL = 1024
BATCH = 4
SEQ = 4096
DEPTH = 4

CHUNK = 64
N_MEM = 256
D_FF = 2816
NORM_EPS = 1e-6
RWKV_HEADS = 8
RWKV_HEAD_DIM = 64
RWKV_WIDTH = RWKV_HEADS * RWKV_HEAD_DIM
DECAY_LORA = 64
AAA_LORA = 64
MV_LORA = 32
GATE_LORA = 128
RWKV_LN_EPS = 64e-5
GLA_HEADS = 4
GLA_DK = 64
GLA_DV = 128
GLA_KW = GLA_HEADS * GLA_DK
GLA_VW = GLA_HEADS * GLA_DV
GLA_GATE_LORA = 16
GLA_TAU = 16.0
CONV_WIDTH = 4
XA_HEADS = 4
XA_HEAD_DIM = 128
XA_WIDTH = XA_HEADS * XA_HEAD_DIM
N_BRANCH = 3
BRANCH_WIDTH = 512
RWKV_COLS = 3 * RWKV_WIDTH + DECAY_LORA + AAA_LORA + GATE_LORA
GLA_QKV_COLS = 2 * GLA_KW + GLA_VW
GLA_COLS = GLA_QKV_COLS + GLA_GATE_LORA + GLA_VW
GATE_COLS = N_BRANCH * D_MODEL
W_IN_COLS = RWKV_COLS + GLA_COLS + XA_WIDTH + GATE_COLS

kernel_name = 'hybrid_rwkv7_gla_memattn_macaron'


def split_cols(t, sizes):
    idx = np.cumsum(sizes)[:-1].tolist()
    return jnp.split(t, idx, axis=-1)


def rmsnorm(x, g):
    xf = x.astype(jnp.float32)
    y = xf * lax.rsqrt(jnp.mean(xf * xf, axis=-1, keepdims=True) + NORM_EPS)
    return (y * g.astype(jnp.float32)).astype(x.dtype)


def shift_right(t):
    return jnp.pad(t, ((0, 0), (1, 0), (0, 0)))[:, :-1]


def causal_depthwise_conv(t, w):
    s = t.shape[1]
    tp = jnp.pad(t, ((0, 0), (CONV_WIDTH - 1, 0), (0, 0)))
    out = tp[:, 0:s] * w[0]
    for j in range(1, CONV_WIDTH):
        out = out + tp[:, j:j + s] * w[j]
    return out


def swiglu_ffn(h, w_in, w_out):
    gate, up = jnp.split(h @ w_in, 2, axis=-1)
    return (jax.nn.silu(gate) * up) @ w_out


def rwkv7_mixer(u, v_first, mu, w0, w2, a0, a2, g2, k_k, k_a, r_k, ln_w, ln_b, v_mix):
    b, s, _ = u.shape
    u = u + mu * (shift_right(u) - u)
    r, k, v, wl, al, gl = split_cols(u, (RWKV_WIDTH, RWKV_WIDTH, RWKV_WIDTH, DECAY_LORA, AAA_LORA, GATE_LORA))
    if v_mix is None:
        v_first = v
    else:
        v0, v1, v2 = v_mix
        v = v + (v_first - v) * jax.nn.sigmoid(v0 + (v @ v1) @ v2)
    w_log = -jax.nn.softplus(-(w0 + jnp.tanh(wl) @ w2)) - 0.5
    a = jax.nn.sigmoid(a0 + al @ a2)
    g = jax.nn.sigmoid(gl) @ g2
    hshape = (RWKV_HEADS, RWKV_HEAD_DIM)
    heads = lambda t: t.astype(jnp.float32).reshape(b, s, *hshape)
    r_h, v_h, a_h = heads(r), heads(v), heads(a)
    kk = heads(k) * k_k.astype(jnp.float32).reshape(hshape)
    kk = kk / jnp.maximum(jnp.linalg.norm(kk, axis=-1, keepdims=True), 1e-12)
    k_h = heads(k) * (1.0 + (a_h - 1.0) * k_a.astype(jnp.float32).reshape(hshape))
    decay = jnp.exp(-jnp.exp(heads(w_log)))

    def step(state, inp):
        r_t, w_t, k_t, v_t, kk_t, a_t = inp
        sa = jnp.einsum('bhvk,bhk->bhv', state, -kk_t)
        state = (state * w_t[:, :, None, :] + sa[..., None] * (kk_t * a_t)[:, :, None, :]
                 + v_t[..., None] * k_t[:, :, None, :])
        return state, jnp.einsum('bhvk,bhk->bhv', state, r_t)

    seq_first = lambda t: jnp.moveaxis(t, 1, 0)
    s0 = jnp.zeros((b, RWKV_HEADS, RWKV_HEAD_DIM, RWKV_HEAD_DIM), jnp.float32)
    _, y = lax.scan(step, s0, tuple(seq_first(t) for t in (r_h, decay, k_h, v_h, kk, a_h)))
    y = jnp.moveaxis(y, 0, 1)
    mean = jnp.mean(y, axis=-1, keepdims=True)
    var = jnp.mean(jnp.square(y - mean), axis=-1, keepdims=True)
    yn = (y - mean) * lax.rsqrt(var + RWKV_LN_EPS) * ln_w.astype(jnp.float32).reshape(hshape) \
        + ln_b.astype(jnp.float32).reshape(hshape)
    bonus = jnp.sum(r_h * k_h * r_k.astype(jnp.float32), axis=-1, keepdims=True) * v_h
    out = (yn + bonus).reshape(b, s, RWKV_WIDTH) * g.astype(jnp.float32)
    return out.astype(u.dtype), v_first


def gla_mixer(u, conv_w, a_up, a_bias, norm_w):
    b, s, _ = u.shape
    n_chunks = s // CHUNK
    qkv, al, go = split_cols(u, (GLA_QKV_COLS, GLA_GATE_LORA, GLA_VW))
    qkv = jax.nn.silu(causal_depthwise_conv(qkv, conv_w))
    q, k, v = split_cols(qkv, (GLA_KW, GLA_KW, GLA_VW))
    log_a = jax.nn.log_sigmoid((al @ a_up + a_bias).astype(jnp.float32)) / GLA_TAU
    chunks = lambda t, d: t.astype(jnp.float32).reshape(b, n_chunks, CHUNK, GLA_HEADS, d).transpose(0, 3, 1, 2, 4)
    q = chunks(q, GLA_DK) * (GLA_DK ** -0.5)
    k = chunks(k, GLA_DK)
    v = chunks(v, GLA_DV)
    gcum = jnp.cumsum(chunks(log_a, GLA_DK), axis=3)
    qg, kg = q * jnp.exp(gcum), k * jnp.exp(-gcum)
    qr, kr = q * jnp.exp(-gcum), k * jnp.exp(gcum)
    a_past = jnp.einsum('bhntd,bhnsd->bhnts', qg, kg)
    a_future = jnp.einsum('bhntd,bhnsd->bhnts', qr, kr)
    lower = jnp.tril(jnp.ones((CHUNK, CHUNK), dtype=bool))
    intra = jnp.einsum('bhnts,bhnse->bhnte', jnp.where(lower, a_past, a_future), v)
    g_last = gcum[:, :, :, -1:, :]
    kv_chunk = jnp.einsum('bhnsd,bhnse->bhnde', k * jnp.exp(g_last - gcum), v)
    decay_chunk = jnp.exp(g_last[:, :, :, 0, :])

    def step(state, inp):
        kv_c, dec_c = inp
        return state * dec_c[..., None] + kv_c, state

    s0 = jnp.zeros((b, GLA_HEADS, GLA_DK, GLA_DV), jnp.float32)
    _, s_prev = lax.scan(step, s0, (jnp.moveaxis(kv_chunk, 2, 0), jnp.moveaxis(decay_chunk, 2, 0)))
    inter = jnp.einsum('bhntd,bhnde->bhnte', qg, jnp.moveaxis(s_prev, 0, 2))
    o = (intra + inter).transpose(0, 2, 3, 1, 4).reshape(b, s, GLA_HEADS, GLA_DV)
    o = o * lax.rsqrt(jnp.mean(o * o, axis=-1, keepdims=True) + NORM_EPS) \
        * norm_w.astype(jnp.float32).reshape(GLA_HEADS, GLA_DV)
    out = o.reshape(b, s, GLA_VW) * jax.nn.silu(go.astype(jnp.float32))
    return out.astype(u.dtype)


def memory_attention(q, mem_n, w_kv):
    b, s, _ = q.shape
    m = mem_n.shape[1]
    k, v = jnp.split(mem_n @ w_kv, 2, axis=-1)
    q = q.reshape(b, s, XA_HEADS, XA_HEAD_DIM)
    k = k.reshape(b, m, XA_HEADS, XA_HEAD_DIM)
    v = v.reshape(b, m, XA_HEADS, XA_HEAD_DIM)
    scores = jnp.einsum('bshd,bmhd->bhsm', q, k).astype(jnp.float32) * (XA_HEAD_DIM ** -0.5)
    p = jax.nn.softmax(scores, axis=-1).astype(v.dtype)
    return jnp.einsum('bhsm,bmhd->bshd', p, v).reshape(b, s, XA_WIDTH)


def setup_inputs(seed: int = 0) -> dict:
    key = jax.random.key(seed)
    ks = iter(jax.random.split(key, 40))
    nrm = lambda shape, scale: scale * jax.random.normal(next(ks), shape, jnp.float32)
    gain = lambda shape: 1.0 + nrm(shape, 0.05)
    L = DEPTH
    return {
        'x': nrm((BATCH, SEQ, D_MODEL), 1.0),
        'mem': nrm((BATCH, N_MEM, D_MODEL), 1.0),
        'ffn1_norm': gain((L, D_MODEL)),
        'ffn1_w_in': nrm((L, D_MODEL, 2 * D_FF), D_MODEL ** -0.5),
        'ffn1_w_out': nrm((L, D_FF, D_MODEL), D_FF ** -0.5),
        'mix_norm': gain((L, D_MODEL)),
        'mem_norm': gain((L, D_MODEL)),
        'w_in': nrm((L, D_MODEL, W_IN_COLS), D_MODEL ** -0.5),
        'rwkv_mu': 0.5 + nrm((L, RWKV_COLS), 0.1),
        'rwkv_w0': -1.0 + nrm((L, RWKV_WIDTH), 0.5),
        'rwkv_w2': nrm((L, DECAY_LORA, RWKV_WIDTH), 0.5 * DECAY_LORA ** -0.5),
        'rwkv_a0': nrm((L, RWKV_WIDTH), 0.1),
        'rwkv_a2': nrm((L, AAA_LORA, RWKV_WIDTH), 0.5 * AAA_LORA ** -0.5),
        'rwkv_g2': nrm((L, GATE_LORA, RWKV_WIDTH), GATE_LORA ** -0.5),
        'rwkv_k_k': 0.85 + nrm((L, RWKV_WIDTH), 0.05),
        'rwkv_k_a': 1.0 + nrm((L, RWKV_WIDTH), 0.05),
        'rwkv_r_k': nrm((L, RWKV_HEADS, RWKV_HEAD_DIM), 0.1),
        'rwkv_ln_w': gain((L, RWKV_WIDTH)),
        'rwkv_ln_b': nrm((L, RWKV_WIDTH), 0.02),
        'rwkv_v0': nrm((L - 1, RWKV_WIDTH), 0.1),
        'rwkv_v1': nrm((L - 1, RWKV_WIDTH, MV_LORA), RWKV_WIDTH ** -0.5),
        'rwkv_v2': nrm((L - 1, MV_LORA, RWKV_WIDTH), 0.5 * MV_LORA ** -0.5),
        'gla_conv': nrm((L, CONV_WIDTH, GLA_QKV_COLS), CONV_WIDTH ** -0.5),
        'gla_a_up': nrm((L, GLA_GATE_LORA, GLA_KW), GLA_GATE_LORA ** -0.5),
        'gla_a_bias': 1.0 + nrm((L, GLA_KW), 0.5),
        'gla_norm': gain((L, GLA_VW)),
        'xa_w_kv': nrm((L, D_MODEL, 2 * XA_WIDTH), D_MODEL ** -0.5),
        'w_branch': nrm((L, N_BRANCH, BRANCH_WIDTH, D_MODEL), BRANCH_WIDTH ** -0.5),
        'w_out': nrm((L, D_MODEL, D_MODEL), D_MODEL ** -0.5),
        'ffn2_norm': gain((L, D_MODEL)),
        'ffn2_w_in': nrm((L, D_MODEL, 2 * D_FF), D_MODEL ** -0.5),
        'ffn2_w_out': nrm((L, D_FF, D_MODEL), D_FF ** -0.5),
        'final_norm': gain((D_MODEL,)),
    }


def reference(x, mem, ffn1_norm, ffn1_w_in, ffn1_w_out, mix_norm, mem_norm, w_in, rwkv_mu, rwkv_w0,
              rwkv_w2, rwkv_a0, rwkv_a2, rwkv_g2, rwkv_k_k, rwkv_k_a, rwkv_r_k, rwkv_ln_w, rwkv_ln_b,
              rwkv_v0, rwkv_v1, rwkv_v2, gla_conv, gla_a_up, gla_a_bias, gla_norm, xa_w_kv, w_branch,
              w_out, ffn2_norm, ffn2_w_in, ffn2_w_out, final_norm):
    b, s, d = x.shape
    v_first = None
    for l in range(DEPTH):
        x = x + 0.5 * swiglu_ffn(rmsnorm(x, ffn1_norm[l]), ffn1_w_in[l], ffn1_w_out[l])
        h = rmsnorm(x, mix_norm[l])
        u_rwkv, u_gla, u_xa, u_gate = split_cols(h @ w_in[l], (RWKV_COLS, GLA_COLS, XA_WIDTH, GATE_COLS))
        v_mix = None if l == 0 else (rwkv_v0[l - 1], rwkv_v1[l - 1], rwkv_v2[l - 1])
        y_rwkv, v_first = rwkv7_mixer(u_rwkv, v_first, rwkv_mu[l], rwkv_w0[l], rwkv_w2[l], rwkv_a0[l],
                                      rwkv_a2[l], rwkv_g2[l], rwkv_k_k[l], rwkv_k_a[l], rwkv_r_k[l],
                                      rwkv_ln_w[l], rwkv_ln_b[l], v_mix)
        y_gla = gla_mixer(u_gla, gla_conv[l], gla_a_up[l], gla_a_bias[l], gla_norm[l])
        y_xa = memory_attention(u_xa, rmsnorm(mem, mem_norm[l]), xa_w_kv[l])
        branches = jnp.stack([y_rwkv, y_gla, y_xa], axis=2)
        gates = jax.nn.sigmoid(u_gate.reshape(b, s, N_BRANCH, d))
        merged = jnp.einsum('bsjc,jcd,bsjd->bsd', branches, w_branch[l], gates)
        x = x + merged @ w_out[l]
        x = x + 0.5 * swiglu_ffn(rmsnorm(x, ffn2_norm[l]), ffn2_w_in[l], ffn2_w_out[l])
    return rmsnorm(x, final_norm)
```

```python
import functools

import jax
import jax.numpy as jnp
from jax import lax
from jax.experimental import pallas as pl
from jax.experimental.pallas import tpu as pltpu

F32 = jnp.float32
BF16 = jnp.bfloat16

NORM_EPS = 1e-6
RWKV_LN_EPS = 64e-5
GLA_TAU = 16.0
CHUNK = 64
HEAD = 64
LANES = 128
VMEM_LIMIT = 56 * 1024 * 1024

NN = ((1,), (0,))
NT = ((1,), (1,))
TN = ((0,), (0,))


def _dot(a, b, dims=NN):
    return lax.dot_general(a, b, (dims, ((), ())), preferred_element_type=F32)


def _split(x, n):
    if x.dtype == BF16:
        return [x]
    parts, rest = [], x
    for i in range(n):
        p = rest.astype(BF16)
        parts.append(p)
        if i + 1 < n:
            rest = rest - p.astype(F32)
    return parts


def _mm(a, b, dims=NN, pa=1, pb=1):
    aa, bb = _split(a, pa), _split(b, pb)
    order = max(len(aa), len(bb))
    out = None
    for i, x in enumerate(aa):
        for j, y in enumerate(bb):
            if i + j < order:
                t = _dot(x, y, dims)
                out = t if out is None else out + t
    return out


def _sigmoid(x):
    return 1.0 / (1.0 + jnp.exp(-x))


def _softplus(x):
    return jnp.maximum(x, 0.0) + jnp.log(1.0 + jnp.exp(-jnp.abs(x)))


def _rms(x, g):
    return x * lax.rsqrt(jnp.mean(x * x, axis=-1, keepdims=True) + NORM_EPS) * g


def _stack_heads(x, lane_lo):
    return jnp.concatenate([jnp.where(lane_lo, x, 0.0), jnp.where(lane_lo, 0.0, x)], axis=0)


def _const_spec(shape):
    nd = len(shape)
    return pl.BlockSpec(shape, lambda *_: (0,) * nd)


def _ffn_kernel(x_ref, g_ref, win_ref, wout_ref, *rest, d_ff, tf, final):
    if final:
        gf_ref, o_ref, acc_ref = rest
    else:
        o_ref, acc_ref = rest
    x = x_ref[...]
    h = _rms(x, g_ref[...]).astype(BF16)
    for c in range(d_ff // tf):
        gate = _dot(h, win_ref[:, c * tf:(c + 1) * tf])
        up = _dot(h, win_ref[:, d_ff + c * tf:d_ff + (c + 1) * tf])
        act = (gate * _sigmoid(gate) * up).astype(BF16)
        part = _dot(act, wout_ref[c * tf:(c + 1) * tf, :])
        if c == 0:
            acc_ref[...] = part
        else:
            acc_ref[...] += part
    y = x + 0.5 * acc_ref[...]
    if final:
        y = _rms(y, gf_ref[...])
    o_ref[...] = y


def _ffn(x, g, w_in, w_out, final_g=None, *, tm=512, tf=256):
    t, d = x.shape
    d_ff = w_out.shape[0]
    final = final_g is not None
    row = pl.BlockSpec((tm, d), lambda i: (i, 0))
    in_specs = [row, _const_spec((1, d)), _const_spec(w_in.shape), _const_spec(w_out.shape)]
    args = [x, g, w_in, w_out]
    if final:
        in_specs.append(_const_spec((1, d)))
        args.append(final_g)
    return pl.pallas_call(
        functools.partial(_ffn_kernel, d_ff=d_ff, tf=tf, final=final),
        grid=(t // tm,),
        in_specs=in_specs,
        out_specs=row,
        out_shape=jax.ShapeDtypeStruct((t, d), F32),
        scratch_shapes=[pltpu.VMEM((tm, d), F32)],
        compiler_params=pltpu.CompilerParams(
            dimension_semantics=("arbitrary",), vmem_limit_bytes=VMEM_LIMIT),
        name="ffn_final" if final else "ffn",
    )(*args)


def _rwkv_chunk_pair(rt, at, bh, kh, bbar, kbar, v, e_gc, s, masks):
    lane_lo, strict, incl, levels, eye, blockdiag = masks
    bst = _stack_heads(bh, lane_lo)
    kst = _stack_heads(kh, lane_lo)
    ar = jnp.concatenate([at, rt], axis=0)
    ab = _mm(ar, bst, NT, 2, 2)
    ak = _mm(ar, kst, NT, 2, 2)
    a_ab = jnp.where(strict, ab[:CHUNK], 0.0)
    a_rb = jnp.where(incl, ab[CHUNK:], 0.0)
    a_ak = jnp.where(strict, ak[:CHUNK], 0.0)
    a_rk = jnp.where(incl, ak[CHUNK:], 0.0)
    t = eye + jnp.where(levels[0], a_ab, 0.0)
    for m in levels[1:]:
        x = _mm(t, _stack_heads(jnp.where(m, a_ab, 0.0), lane_lo), NN, 2, 2)
        t = t + _mm(x, _stack_heads(t, lane_lo), NN, 2, 2)
    vst = _stack_heads(v, lane_lo)
    w = _mm(at, s, NT, 2, 2)
    av = _mm(a_ak, vst, NN, 2, 2)
    u = _mm(t, _stack_heads(w + av, lane_lo), NN, 2, 2)
    y = _mm(rt, s, NT, 2, 2) + _mm(a_rb, _stack_heads(u, lane_lo), NN, 2, 2) + _mm(a_rk, vst, NN, 2, 2)
    upd = _mm(u, bbar, TN, 2, 2) + _mm(v, kbar, TN, 2, 2)
    s_new = s * e_gc + jnp.where(blockdiag, upd, 0.0)
    return y, s_new


def _pair_masks():
    row = lax.broadcasted_iota(jnp.int32, (CHUNK, LANES), 0)
    col = lax.broadcasted_iota(jnp.int32, (CHUNK, LANES), 1) % HEAD
    lane_lo = lax.broadcasted_iota(jnp.int32, (1, LANES), 1) < HEAD
    strict = row > col
    incl = row >= col
    levels = []
    for k in range(6):
        levels.append(((row >> (k + 1)) == (col >> (k + 1)))
                      & (((row >> k) & 1) == 1) & (((col >> k) & 1) == 0))
    eye = (row == col).astype(F32)
    r2 = lax.broadcasted_iota(jnp.int32, (LANES, LANES), 0) < HEAD
    c2 = lax.broadcasted_iota(jnp.int32, (LANES, LANES), 1) < HEAD
    blockdiag = r2 == c2
    return lane_lo, strict, incl, levels, eye, blockdiag


def _tri_ones():
    r = lax.broadcasted_iota(jnp.int32, (CHUNK, CHUNK), 0)
    c = lax.broadcasted_iota(jnp.int32, (CHUNK, CHUNK), 1)
    return (r >= c).astype(BF16)


def _rwkv_kernel(*refs, ts, width, has_vmix):
    (x_ref, gmix_ref, wr_ref, mu_ref, wl_ref, bl_ref, g2_ref, kkw_ref, ka_ref, rk_ref,
     lnw_ref, lnb_ref, ones_ref) = refs[:13]
    refs = refs[13:]
    if has_vmix:
        v0_ref, v1_ref, v2_ref, vf_ref = refs[:4]
        refs = refs[4:]
        y_ref = refs[0]
        refs = refs[1:]
    else:
        y_ref, vf_out_ref = refs[:2]
        refs = refs[2:]
    uprev, st, r_s, k_s, v_s, kk_s, b_s, lw_s, g_s, bg_s, y_s = refs

    @pl.when(pl.program_id(1) == 0)
    def _():
        uprev[...] = jnp.zeros_like(uprev)
        st[...] = jnp.zeros_like(st)

    x = x_ref[0]
    h = _rms(x, gmix_ref[...]).astype(BF16)
    u = _dot(h, wr_ref[...])
    rows = lax.broadcasted_iota(jnp.int32, (ts, 1), 0)
    u_prev = jnp.where(rows == 0, uprev[0:1, :], pltpu.roll(u, 1, 0))
    uprev[...] = pltpu.roll(u[ts - 8:ts, :], 1, 0)
    us = u + mu_ref[...] * (u_prev - u)
    r = us[:, 0:width]
    k = us[:, width:2 * width]
    v = us[:, 2 * width:3 * width]
    wa = us[:, 3 * width:3 * width + LANES]
    gl = us[:, 3 * width + LANES:3 * width + 2 * LANES]
    if has_vmix:
        z = _mm(_mm(v, v1_ref[...], NN, 2, 1), v2_ref[...], NN, 2, 1)
        v = v + (vf_ref[0] - v) * _sigmoid(v0_ref[...] + z)
    else:
        vf_out_ref[0] = v
    lane = lax.broadcasted_iota(jnp.int32, (1, LANES), 1)
    act = jnp.where(lane < HEAD, jnp.tanh(wa), wa)
    lo = _mm(act, wl_ref[...], NN, 2, 1) + bl_ref[...]
    w_log = -_softplus(-lo[:, :width]) - 0.5
    lw = -jnp.exp(w_log)
    a = _sigmoid(lo[:, width:])
    g = _mm(_sigmoid(gl), g2_ref[...], NN, 2, 1)
    ones = ones_ref[...]
    kk = k * kkw_ref[...]
    ss = _mm(kk * kk, ones, NN, 2, 1)
    kk = kk / jnp.maximum(jnp.sqrt(ss), 1e-12)
    kh = k * (1.0 + (a - 1.0) * ka_ref[...])
    bonus = _mm(r * kh * rk_ref[...], ones, NN, 2, 1) * v
    r_s[...] = r
    k_s[...] = kh
    v_s[...] = v
    kk_s[...] = kk
    b_s[...] = kk * a
    lw_s[...] = lw
    g_s[...] = g
    bg_s[...] = bonus * g

    masks = _pair_masks()
    tri = _tri_ones()

    def chunk(c, carry):
        off = pl.multiple_of(c * CHUNK, CHUNK)
        sl = pl.ds(off, CHUNK)
        lwc = lw_s[sl, :]
        gc = _mm(tri, lwc, NN, 1, 3)
        e_g = jnp.exp(gc)
        e_ng = jnp.exp(-gc)
        e_gm = jnp.exp(gc - lwc)
        e_gc = jnp.exp(gc[CHUNK - 1:CHUNK, :])
        rt = r_s[sl, :] * e_g
        at = -kk_s[sl, :] * e_gm
        bh = b_s[sl, :] * e_ng
        kh_ = k_s[sl, :] * e_ng
        bbar = bh * e_gc
        kbar = kh_ * e_gc
        vc = v_s[sl, :]
        for p in range(width // LANES):
            ls = slice(p * LANES, (p + 1) * LANES)
            y, s_new = _rwkv_chunk_pair(rt[:, ls], at[:, ls], bh[:, ls], kh_[:, ls], bbar[:, ls],
                                        kbar[:, ls], vc[:, ls], e_gc[:, ls], st[p], masks)
            y_s[sl, ls] = y
            st[p] = s_new
        return carry

    lax.fori_loop(0, ts // CHUNK, chunk, 0)

    y = y_s[...]
    inv_n = 1.0 / HEAD
    mean = _mm(y, ones, NN, 2, 1) * inv_n
    d = y - mean
    var = _mm(d * d, ones, NN, 2, 1) * inv_n
    yn = d * lax.rsqrt(var + RWKV_LN_EPS) * lnw_ref[...] + lnb_ref[...]
    y_ref[0] = yn * g_s[...] + bg_s[...]


def _rwkv(x3, p, vmix, v_first, *, ts=512):
    b, s, d = x3.shape
    width = p["kkw"].shape[-1]
    has_vmix = vmix is not None
    tile = lambda w: pl.BlockSpec((1, ts, w), lambda i, j: (i, j, 0))
    names = ("gmix", "wr", "mu", "wl", "bl", "g2", "kkw", "ka", "rk", "lnw", "lnb", "ones")
    args = [x3] + [p[n] for n in names]
    in_specs = [tile(d)] + [_const_spec(p[n].shape) for n in names]
    if has_vmix:
        args += list(vmix) + [v_first]
        in_specs += [_const_spec(a.shape) for a in vmix] + [tile(width)]
        out_shape = jax.ShapeDtypeStruct((b, s, width), F32)
        out_specs = tile(width)
    else:
        out_shape = (jax.ShapeDtypeStruct((b, s, width), F32),) * 2
        out_specs = (tile(width), tile(width))
    ucols = p["wr"].shape[1]
    tw = pltpu.VMEM((ts, width), F32)
    scratch = [pltpu.VMEM((8, ucols), F32), pltpu.VMEM((width // LANES, LANES, LANES), F32)] + [tw] * 9
    return pl.pallas_call(
        functools.partial(_rwkv_kernel, ts=ts, width=width, has_vmix=has_vmix),
        grid=(b, s // ts),
        in_specs=in_specs,
        out_specs=out_specs,
        out_shape=out_shape,
        scratch_shapes=scratch,
        compiler_params=pltpu.CompilerParams(
            dimension_semantics=("arbitrary", "arbitrary"), vmem_limit_bytes=VMEM_LIMIT),
        name="rwkv_vmix" if has_vmix else "rwkv",
    )(*args)


def _gla_kernel(x_ref, gmix_ref, wg_ref, conv_ref, aup_ref, ab_ref, nw_ref, y_ref,
                carry, st, q_s, k_s, v_s, la_s, o_s, *, ts, kw, vw, n_heads):
    dv = vw // n_heads
    qkv_w = 2 * kw + vw

    @pl.when(pl.program_id(1) == 0)
    def _():
        carry[...] = jnp.zeros_like(carry)
        st[...] = jnp.zeros_like(st)

    x = x_ref[0]
    h = _rms(x, gmix_ref[...]).astype(BF16)
    u = _dot(h, wg_ref[...])
    qkv = u[:, :qkv_w]
    go = u[:, qkv_w:qkv_w + vw]
    al = u[:, qkv_w + vw:qkv_w + vw + LANES]
    rows = lax.broadcasted_iota(jnp.int32, (ts, 1), 0)
    prev8 = carry[...]
    taps = conv_ref.shape[0]
    out = qkv * conv_ref[taps - 1:taps, :]
    for j in range(1, taps):
        shifted = pltpu.roll(qkv, j, 0)
        fill = pltpu.roll(prev8, j, 0)
        first = jnp.where(rows[:8] < j, fill, shifted[:8])
        shifted = jnp.concatenate([first, shifted[8:]], axis=0)
        out = out + shifted * conv_ref[taps - 1 - j:taps - j, :]
    carry[...] = qkv[ts - 8:ts, :]
    qkv = out * _sigmoid(out)
    log_a = _mm(al, aup_ref[...], NN, 2, 1) + ab_ref[...]
    log_a = (jnp.minimum(log_a, 0.0) - jnp.log(1.0 + jnp.exp(-jnp.abs(log_a)))) * (1.0 / GLA_TAU)
    q_s[...] = qkv[:, :kw] * (float(kw // n_heads) ** -0.5)
    k_s[...] = qkv[:, kw:2 * kw]
    v_s[...] = qkv[:, 2 * kw:]
    la_s[...] = log_a

    lane_lo = lax.broadcasted_iota(jnp.int32, (1, LANES), 1) < HEAD
    r = lax.broadcasted_iota(jnp.int32, (CHUNK, CHUNK), 0)
    c = lax.broadcasted_iota(jnp.int32, (CHUNK, CHUNK), 1)
    lower = r >= c
    tri = lower.astype(BF16)

    def chunk(ci, cr):
        off = pl.multiple_of(ci * CHUNK, CHUNK)
        sl = pl.ds(off, CHUNK)
        gc = _mm(tri, la_s[sl, :], NN, 1, 3)
        e_g = jnp.exp(gc)
        e_ng = jnp.exp(-gc)
        e_last = jnp.exp(gc[CHUNK - 1:CHUNK, :])
        q = q_s[sl, :]
        k = k_s[sl, :]
        qg, qr = q * e_g, q * e_ng
        kg, kr = k * e_ng, k * e_g
        kbar = kg * e_last
        vc = v_s[sl, :]
        for pr in range(kw // LANES):
            ls = slice(pr * LANES, (pr + 1) * LANES)
            a_past = _mm(_stack_heads(qg[:, ls], lane_lo), kg[:, ls], NT)
            a_fut = _mm(_stack_heads(qr[:, ls], lane_lo), kr[:, ls], NT)
            for hh in range(2):
                hd = 2 * pr + hh
                a = jnp.where(lower, a_past[hh * CHUNK:(hh + 1) * CHUNK], a_fut[hh * CHUNK:(hh + 1) * CHUNK])
                vh = vc[:, hd * dv:(hd + 1) * dv]
                s_h = st[hd]
                o = _mm(a, vh) + _mm(qg[:, ls], s_h, NT)
                own = lane_lo if hh == 0 else jnp.logical_not(lane_lo)
                st[hd] = s_h * e_last[:, ls] + _mm(vh, jnp.where(own, kbar[:, ls], 0.0), TN)
                o_s[sl, hd * dv:(hd + 1) * dv] = o
        return cr

    lax.fori_loop(0, ts // CHUNK, chunk, 0)

    for hd in range(n_heads):
        hs = slice(hd * dv, (hd + 1) * dv)
        o = o_s[:, hs]
        o = o * lax.rsqrt(jnp.mean(o * o, axis=-1, keepdims=True) + NORM_EPS) * nw_ref[:, hs]
        gate = go[:, hs]
        y_ref[0, :, hs] = o * (gate * _sigmoid(gate))


def _gla(x3, p, *, ts=512):
    b, s, d = x3.shape
    kw = p["aup"].shape[1]
    vw = p["nw"].shape[1]
    n_heads = kw // HEAD
    tile = lambda w: pl.BlockSpec((1, ts, w), lambda i, j: (i, j, 0))
    names = ("gmix", "wg", "conv", "aup", "ab", "nw")
    tkw = pltpu.VMEM((ts, kw), F32)
    tvw = pltpu.VMEM((ts, vw), F32)
    return pl.pallas_call(
        functools.partial(_gla_kernel, ts=ts, kw=kw, vw=vw, n_heads=n_heads),
        grid=(b, s // ts),
        in_specs=[tile(d)] + [_const_spec(p[n].shape) for n in names],
        out_specs=tile(vw),
        out_shape=jax.ShapeDtypeStruct((b, s, vw), F32),
        scratch_shapes=[pltpu.VMEM((8, 2 * kw + vw), F32), pltpu.VMEM((n_heads, vw // n_heads, LANES), F32),
                        tkw, tkw, tvw, tkw, tvw],
        compiler_params=pltpu.CompilerParams(
            dimension_semantics=("arbitrary", "arbitrary"), vmem_limit_bytes=VMEM_LIMIT),
        name="gla",
    )(x3, *[p[n] for n in names])


def _xa_kv_kernel(mem_ref, g_ref, wkv_ref, k_ref, v_ref, *, width):
    m = _rms(mem_ref[0], g_ref[...]).astype(BF16)
    kv = _dot(m, wkv_ref[...])
    k_ref[0] = kv[:, :width].astype(BF16)
    v_ref[0] = kv[:, width:].astype(BF16)


def _xa_kv(mem, g, wkv):
    b, m, d = mem.shape
    width = wkv.shape[1] // 2
    blk = lambda w: pl.BlockSpec((1, m, w), lambda i: (i, 0, 0))
    return pl.pallas_call(
        functools.partial(_xa_kv_kernel, width=width),
        grid=(b,),
        in_specs=[blk(d), _const_spec(g.shape), _const_spec(wkv.shape)],
        out_specs=(blk(width), blk(width)),
        out_shape=(jax.ShapeDtypeStruct((b, m, width), BF16),) * 2,
        compiler_params=pltpu.CompilerParams(dimension_semantics=("arbitrary",), vmem_limit_bytes=VMEM_LIMIT),
        name="xa_kv",
    )(mem, g, wkv)


def _xa_kernel(x_ref, gmix_ref, wq_ref, k_ref, v_ref, y_ref, *, n_heads, hd):
    h = _rms(x_ref[0], gmix_ref[...]).astype(BF16)
    q = _dot(h, wq_ref[...]) * (float(hd) ** -0.5)
    for i in range(n_heads):
        hs = slice(i * hd, (i + 1) * hd)
        sc = _dot(q[:, hs].astype(BF16), k_ref[0, :, hs], NT)
        sc = sc - jnp.max(sc, axis=-1, keepdims=True)
        e = jnp.exp(sc)
        pr = e / jnp.sum(e, axis=-1, keepdims=True)
        y_ref[0, :, hs] = _dot(pr.astype(BF16), v_ref[0, :, hs])


def _xa(x3, gmix, wq, k, v, *, ts=512, hd=128):
    b, s, d = x3.shape
    m, width = k.shape[1], k.shape[2]
    tile = lambda w: pl.BlockSpec((1, ts, w), lambda i, j: (i, j, 0))
    kvs = pl.BlockSpec((1, m, width), lambda i, j: (i, 0, 0))
    return pl.pallas_call(
        functools.partial(_xa_kernel, n_heads=width // hd, hd=hd),
        grid=(b, s // ts),
        in_specs=[tile(d), _const_spec(gmix.shape), _const_spec(wq.shape), kvs, kvs],
        out_specs=tile(width),
        out_shape=jax.ShapeDtypeStruct((b, s, width), F32),
        compiler_params=pltpu.CompilerParams(
            dimension_semantics=("arbitrary", "arbitrary"), vmem_limit_bytes=VMEM_LIMIT),
        name="xa",
    )(x3, gmix, wq, k, v)


def _merge_kernel(x_ref, yr_ref, yg_ref, yx_ref, gmix_ref, wgate_ref, wb_ref, wo_ref, o_ref):
    x = x_ref[...]
    d = x.shape[-1]
    h = _rms(x, gmix_ref[...]).astype(BF16)
    merged = None
    for j, y_ref in enumerate((yr_ref, yg_ref, yx_ref)):
        gate = _sigmoid(_dot(h, wgate_ref[:, j * d:(j + 1) * d]))
        t = _dot(y_ref[...].astype(BF16), wb_ref[j]) * gate
        merged = t if merged is None else merged + t
    o_ref[...] = x + _dot(merged.astype(BF16), wo_ref[...])


def _merge(x, yr, yg, yx, gmix, wgate, wb, wo, *, tm=512):
    t, d = x.shape
    bw = yr.shape[1]
    row = lambda w: pl.BlockSpec((tm, w), lambda i: (i, 0))
    return pl.pallas_call(
        _merge_kernel,
        grid=(t // tm,),
        in_specs=[row(d), row(bw), row(bw), row(bw), _const_spec(gmix.shape), _const_spec(wgate.shape),
                  _const_spec(wb.shape), _const_spec(wo.shape)],
        out_specs=row(d),
        out_shape=jax.ShapeDtypeStruct((t, d), F32),
        compiler_params=pltpu.CompilerParams(dimension_semantics=("arbitrary",), vmem_limit_bytes=VMEM_LIMIT),
        name="merge",
    )(x, yr, yg, yx, gmix, wgate, wb, wo)


def _block_ones(width, block):
    i = jnp.arange(width) // block
    return (i[:, None] == i[None, :]).astype(BF16)


def kernel(x, mem, ffn1_norm, ffn1_w_in, ffn1_w_out, mix_norm, mem_norm, w_in, rwkv_mu, rwkv_w0, rwkv_w2, rwkv_a0, rwkv_a2, rwkv_g2, rwkv_k_k, rwkv_k_a, rwkv_r_k, rwkv_ln_w, rwkv_ln_b, rwkv_v0, rwkv_v1, rwkv_v2, gla_conv, gla_a_up, gla_a_bias, gla_norm, xa_w_kv, w_branch, w_out, ffn2_norm, ffn2_w_in, ffn2_w_out, final_norm):
    b, s, d = x.shape
    depth = w_in.shape[0]
    rw = rwkv_w0.shape[1]
    dl, al_ = rwkv_w2.shape[1], rwkv_a2.shape[1]
    gl_ = rwkv_g2.shape[1]
    kw = gla_a_up.shape[2]
    vw = gla_norm.shape[1]
    glora = gla_a_up.shape[1]
    xw = xa_w_kv.shape[2] // 2
    rcols = 3 * rw + dl + al_ + gl_
    gcols = 2 * kw + vw + glora + vw
    assert dl + al_ == LANES and gl_ == LANES and dl == HEAD

    row1 = lambda a: a[:, None, :]
    w_r = w_in[:, :, :rcols].astype(BF16)
    g0 = rcols
    w_gqkv = w_in[:, :, g0:g0 + 2 * kw + vw]
    w_gal = w_in[:, :, g0 + 2 * kw + vw:g0 + 2 * kw + vw + glora]
    w_ggo = w_in[:, :, g0 + 2 * kw + vw + glora:g0 + gcols]
    w_g = jnp.concatenate([w_gqkv, w_ggo, jnp.pad(w_gal, ((0, 0), (0, 0), (0, LANES - glora)))], axis=-1).astype(BF16)
    w_xq = w_in[:, :, g0 + gcols:g0 + gcols + xw].astype(BF16)
    w_gate = w_in[:, :, g0 + gcols + xw:].astype(BF16)
    zl = jnp.zeros((depth, dl, rw), F32)
    w_lora = jnp.concatenate([jnp.concatenate([rwkv_w2, zl], axis=-1),
                              jnp.concatenate([zl, rwkv_a2], axis=-1)], axis=1).astype(BF16)
    b_lora = row1(jnp.concatenate([rwkv_w0, rwkv_a0], axis=-1))
    mv = rwkv_v1.shape[2]
    v1p = jnp.pad(rwkv_v1, ((0, 0), (0, 0), (0, LANES - mv))).astype(BF16)
    v2p = jnp.pad(rwkv_v2, ((0, 0), (0, LANES - mv), (0, 0))).astype(BF16)
    aup = jnp.pad(gla_a_up, ((0, 0), (0, LANES - glora), (0, 0))).astype(BF16)
    ones = _block_ones(rw, HEAD)
    f1i, f1o = ffn1_w_in.astype(BF16), ffn1_w_out.astype(BF16)
    f2i, f2o = ffn2_w_in.astype(BF16), ffn2_w_out.astype(BF16)
    wkv = xa_w_kv.astype(BF16)
    wb = w_branch.astype(BF16)
    wo = w_out.astype(BF16)
    g2 = rwkv_g2.astype(BF16)

    xf = x.reshape(b * s, d)
    v_first = None
    for l in range(depth):
        gmix = mix_norm[l][None, :]
        xf = _ffn(xf, ffn1_norm[l][None, :], f1i[l], f1o[l])
        x3 = xf.reshape(b, s, d)
        rp = dict(gmix=gmix, wr=w_r[l], mu=rwkv_mu[l][None, :], wl=w_lora[l], bl=b_lora[l], g2=g2[l],
                  kkw=rwkv_k_k[l][None, :], ka=rwkv_k_a[l][None, :], rk=rwkv_r_k[l].reshape(1, rw),
                  lnw=rwkv_ln_w[l][None, :], lnb=rwkv_ln_b[l][None, :], ones=ones)
        if l == 0:
            y_r, v_first = _rwkv(x3, rp, None, None)
        else:
            y_r = _rwkv(x3, rp, (rwkv_v0[l - 1][None, :], v1p[l - 1], v2p[l - 1]), v_first)
        gp = dict(gmix=gmix, wg=w_g[l], conv=gla_conv[l], aup=aup[l], ab=gla_a_bias[l][None, :],
                  nw=gla_norm[l][None, :])
        y_g = _gla(x3, gp)
        k_m, v_m = _xa_kv(mem, mem_norm[l][None, :], wkv[l])
        y_x = _xa(x3, gmix, w_xq[l], k_m, v_m)
        xf = _merge(xf, y_r.reshape(b * s, rw), y_g.reshape(b * s, vw), y_x.reshape(b * s, xw),
                    gmix, w_gate[l], wb[l], wo[l])
        fin = final_norm[None, :] if l == depth - 1 else None
        xf = _ffn(xf, ffn2_norm[l][None, :], f2i[l], f2o[l], fin)
    return xf.reshape(b, s, d)
```

```python
import functools

import jax
import jax.numpy as jnp
from jax import lax
from jax.experimental import pallas as pl
from jax.experimental.pallas import tpu as pltpu

F32 = jnp.float32
BF16 = jnp.bfloat16

NORM_EPS = 1e-6
RWKV_LN_EPS = 64e-5
GLA_TAU = 16.0
CHUNK = 64
HEAD = 64
LANES = 128
GROUP = 2
VMEM_LIMIT = 56 * 1024 * 1024

NN = ((1,), (0,))
NT = ((1,), (1,))
TN = ((0,), (0,))


def _dot(a, b, dims=NN):
    return lax.dot_general(a, b, (dims, ((), ())), preferred_element_type=F32)


def _split(x, n):
    if x.dtype == BF16:
        return [x]
    parts, rest = [], x
    for i in range(n):
        p = rest.astype(BF16)
        parts.append(p)
        if i + 1 < n:
            rest = rest - p.astype(F32)
    return parts


def _mm(a, b, dims=NN, pa=1, pb=1):
    aa, bb = _split(a, pa), _split(b, pb)
    order = max(len(aa), len(bb))
    out = None
    for i, x in enumerate(aa):
        for j, y in enumerate(bb):
            if i + j < order:
                t = _dot(x, y, dims)
                out = t if out is None else out + t
    return out


def _sigmoid(x):
    return 1.0 / (1.0 + jnp.exp(-x))


def _softplus(x):
    return jnp.maximum(x, 0.0) + jnp.log(1.0 + jnp.exp(-jnp.abs(x)))


def _rms(x, g):
    return x * lax.rsqrt(jnp.mean(x * x, axis=-1, keepdims=True) + NORM_EPS) * g


def _stack_heads(x, lane_lo):
    x = x.astype(BF16)
    zero = jnp.zeros_like(x)
    return jnp.concatenate([jnp.where(lane_lo, x, zero), jnp.where(lane_lo, zero, x)], axis=0)


def _const_spec(shape):
    nd = len(shape)
    return pl.BlockSpec(shape, lambda *_: (0,) * nd)


def _ffn_kernel(x_ref, g_ref, win_ref, wout_ref, *rest, d_ff, tf, final):
    if final:
        gf_ref, o_ref, acc_ref = rest
    else:
        o_ref, acc_ref = rest
    x = x_ref[...]
    h = _rms(x, g_ref[...]).astype(BF16)
    for c in range(d_ff // tf):
        gate = _dot(h, win_ref[:, c * tf:(c + 1) * tf])
        up = _dot(h, win_ref[:, d_ff + c * tf:d_ff + (c + 1) * tf])
        act = (gate * _sigmoid(gate) * up).astype(BF16)
        part = _dot(act, wout_ref[c * tf:(c + 1) * tf, :])
        if c == 0:
            acc_ref[...] = part
        else:
            acc_ref[...] += part
    y = x + 0.5 * acc_ref[...]
    if final:
        y = _rms(y, gf_ref[...])
    o_ref[...] = y


def _ffn(x, g, w_in, w_out, final_g=None, *, tm=512, tf=256):
    t, d = x.shape
    d_ff = w_out.shape[0]
    final = final_g is not None
    row = pl.BlockSpec((tm, d), lambda i: (i, 0))
    in_specs = [row, _const_spec((1, d)), _const_spec(w_in.shape), _const_spec(w_out.shape)]
    args = [x, g, w_in, w_out]
    if final:
        in_specs.append(_const_spec((1, d)))
        args.append(final_g)
    return pl.pallas_call(
        functools.partial(_ffn_kernel, d_ff=d_ff, tf=tf, final=final),
        grid=(t // tm,),
        in_specs=in_specs,
        out_specs=row,
        out_shape=jax.ShapeDtypeStruct((t, d), F32),
        scratch_shapes=[pltpu.VMEM((tm, d), F32)],
        compiler_params=pltpu.CompilerParams(
            dimension_semantics=("arbitrary",), vmem_limit_bytes=VMEM_LIMIT),
        name="ffn_final" if final else "ffn",
    )(*args)


def _rwkv_phase1(probs, masks):
    lane_lo, strict, incl, levels, eye, blockdiag = masks
    n = range(len(probs))
    ar = [jnp.concatenate([q["at"], q["rt"]], axis=0).astype(BF16) for q in probs]
    ab = [_dot(ar[i], _stack_heads(probs[i]["bh"], lane_lo), NT) for i in n]
    ak = [_dot(ar[i], _stack_heads(probs[i]["kh"], lane_lo), NT) for i in n]
    a_ab = [jnp.where(strict, ab[i][:CHUNK], 0.0) for i in n]
    a_rb = [jnp.where(incl, ab[i][CHUNK:], 0.0).astype(BF16) for i in n]
    a_ak = [jnp.where(strict, ak[i][:CHUNK], 0.0).astype(BF16) for i in n]
    a_rk = [jnp.where(incl, ak[i][CHUNK:], 0.0).astype(BF16) for i in n]
    t = [eye + jnp.where(levels[0], a_ab[i], 0.0) for i in n]
    for m in levels[1:]:
        x = [_dot(t[i].astype(BF16), _stack_heads(jnp.where(m, a_ab[i], 0.0), lane_lo)) for i in n]
        t = [t[i] + _dot(x[i].astype(BF16), _stack_heads(t[i], lane_lo)) for i in n]
    vst = [_stack_heads(q["v"], lane_lo) for q in probs]
    av = [_dot(a_ak[i], vst[i]) for i in n]
    arkv = [_dot(a_rk[i], vst[i]) for i in n]
    vk = [jnp.where(blockdiag, _dot(probs[i]["v"].astype(BF16), probs[i]["kbar"].astype(BF16), TN), 0.0)
          for i in n]
    return dict(ar=ar, t=[x.astype(BF16) for x in t], arb=a_rb, av=av, arkv=arkv, vk=vk,
                bbar=[q["bbar"].astype(BF16) for q in probs])


def _pair_masks():
    row = lax.broadcasted_iota(jnp.int32, (CHUNK, LANES), 0)
    col = lax.broadcasted_iota(jnp.int32, (CHUNK, LANES), 1) % HEAD
    lane_lo = lax.broadcasted_iota(jnp.int32, (1, LANES), 1) < HEAD
    strict = row > col
    incl = row >= col
    levels = []
    for k in range(6):
        levels.append(((row >> (k + 1)) == (col >> (k + 1)))
                      & (((row >> k) & 1) == 1) & (((col >> k) & 1) == 0))
    eye = (row == col).astype(F32)
    r2 = lax.broadcasted_iota(jnp.int32, (LANES, LANES), 0) < HEAD
    c2 = lax.broadcasted_iota(jnp.int32, (LANES, LANES), 1) < HEAD
    blockdiag = r2 == c2
    return lane_lo, strict, incl, levels, eye, blockdiag


def _tri_ones():
    r = lax.broadcasted_iota(jnp.int32, (CHUNK, CHUNK), 0)
    c = lax.broadcasted_iota(jnp.int32, (CHUNK, CHUNK), 1)
    return (r >= c).astype(BF16)


def _rwkv_kernel(*refs, ts, width, has_vmix):
    (x_ref, gmix_ref, wr_ref, mu_ref, wl_ref, bl_ref, g2_ref, kkw_ref, ka_ref, rk_ref,
     lnw_ref, lnb_ref, ones_ref) = refs[:13]
    refs = refs[13:]
    if has_vmix:
        v0_ref, v1_ref, v2_ref, vf_ref = refs[:4]
        refs = refs[4:]
        y_ref = refs[0]
        refs = refs[1:]
    else:
        y_ref, vf_out_ref = refs[:2]
        refs = refs[2:]
    (uprev, st, r_s, k_s, v_s, kk_s, b_s, lw_s, g_s, bg_s, y_s,
     ar_s, t_s, arb_s, bbar_s, av_s, arkv_s, vk_s, egc_s) = refs
    n_pairs = width // LANES

    @pl.when(pl.program_id(1) == 0)
    def _():
        uprev[...] = jnp.zeros_like(uprev)
        st[...] = jnp.zeros_like(st)

    x = x_ref[0]
    h = _rms(x, gmix_ref[...]).astype(BF16)
    u = _dot(h, wr_ref[...])
    rows = lax.broadcasted_iota(jnp.int32, (ts, 1), 0)
    u_prev = jnp.where(rows == 0, uprev[0:1, :], pltpu.roll(u, 1, 0))
    uprev[...] = pltpu.roll(u[ts - 8:ts, :], 1, 0)
    us = u + mu_ref[...] * (u_prev - u)
    r = us[:, 0:width]
    k = us[:, width:2 * width]
    v = us[:, 2 * width:3 * width]
    wa = us[:, 3 * width:3 * width + LANES]
    gl = us[:, 3 * width + LANES:3 * width + 2 * LANES]
    if has_vmix:
        z = _mm(_mm(v, v1_ref[...], NN, 2, 1), v2_ref[...], NN, 2, 1)
        v = v + (vf_ref[0] - v) * _sigmoid(v0_ref[...] + z)
    else:
        vf_out_ref[0] = v
    lane = lax.broadcasted_iota(jnp.int32, (1, LANES), 1)
    act = jnp.where(lane < HEAD, jnp.tanh(wa), wa)
    lo = _mm(act, wl_ref[...], NN, 2, 1) + bl_ref[...]
    w_log = -_softplus(-lo[:, :width]) - 0.5
    lw = -jnp.exp(w_log)
    a = _sigmoid(lo[:, width:])
    g = _mm(_sigmoid(gl), g2_ref[...], NN, 2, 1)
    ones = ones_ref[...]
    kk = k * kkw_ref[...]
    ss = _mm(kk * kk, ones)
    kk = kk / jnp.maximum(jnp.sqrt(ss), 1e-12)
    kh = k * (1.0 + (a - 1.0) * ka_ref[...])
    bonus = _mm(r * kh * rk_ref[...], ones) * v
    r_s[...] = r
    k_s[...] = kh
    v_s[...] = v
    kk_s[...] = kk
    b_s[...] = kk * a
    lw_s[...] = lw
    g_s[...] = g
    bg_s[...] = bonus * g

    masks = _pair_masks()
    tri = _tri_ones()

    lane_lo, blockdiag = masks[0], masks[5]

    def phase1(gi, carry):
        probs = []
        for j in range(GROUP):
            c = gi * GROUP + j
            sl = pl.ds(pl.multiple_of(c * CHUNK, CHUNK), CHUNK)
            lwc = lw_s[sl, :]
            gc = _mm(tri, lwc, NN, 1, 3)
            e_g = jnp.exp(gc)
            e_ng = jnp.exp(-gc)
            e_gm = jnp.exp(gc - lwc)
            e_gc = jnp.exp(gc[CHUNK - 1:CHUNK, :])
            egc_s[c] = jnp.broadcast_to(e_gc, (8, width))
            rt = r_s[sl, :] * e_g
            at = -kk_s[sl, :] * e_gm
            bh = b_s[sl, :] * e_ng
            kh_ = k_s[sl, :] * e_ng
            bbar = bh * e_gc
            kbar = kh_ * e_gc
            vc = v_s[sl, :]
            for p in range(n_pairs):
                ls = slice(p * LANES, (p + 1) * LANES)
                probs.append(dict(c=c, p=p, rt=rt[:, ls], at=at[:, ls], bh=bh[:, ls], kh=kh_[:, ls],
                                  bbar=bbar[:, ls], kbar=kbar[:, ls], v=vc[:, ls]))
        res = _rwkv_phase1(probs, masks)
        for i, q in enumerate(probs):
            c, p = q["c"], q["p"]
            ar_s[c, p] = res["ar"][i]
            t_s[c, p] = res["t"][i]
            arb_s[c, p] = res["arb"][i]
            bbar_s[c, p] = res["bbar"][i]
            av_s[c, p] = res["av"][i]
            arkv_s[c, p] = res["arkv"][i]
            vk_s[c, p] = res["vk"][i]
        return carry

    lax.fori_loop(0, ts // (CHUNK * GROUP), phase1, 0)

    def phase2(c, carry):
        sl = pl.ds(pl.multiple_of(c * CHUNK, CHUNK), CHUNK)
        pairs = range(n_pairs)
        s = [st[p] for p in pairs]
        wy = [_dot(ar_s[c, p], s[p].astype(BF16), NT) for p in pairs]
        u = [_dot(t_s[c, p], _stack_heads(wy[p][:CHUNK] + av_s[c, p], lane_lo)) for p in pairs]
        y = [wy[p][CHUNK:] + _dot(arb_s[c, p], _stack_heads(u[p], lane_lo)) + arkv_s[c, p] for p in pairs]
        e_gc = egc_s[c][0:1, :]
        for p in pairs:
            ls = slice(p * LANES, (p + 1) * LANES)
            upd = _dot(u[p].astype(BF16), bbar_s[c, p], TN)
            st[p] = s[p] * e_gc[:, ls] + jnp.where(blockdiag, upd, 0.0) + vk_s[c, p]
            y_s[sl, ls] = y[p]
        return carry

    lax.fori_loop(0, ts // CHUNK, phase2, 0)

    y = y_s[...]
    inv_n = 1.0 / HEAD
    mean = _mm(y, ones) * inv_n
    d = y - mean
    var = _mm(d * d, ones) * inv_n
    yn = d * lax.rsqrt(var + RWKV_LN_EPS) * lnw_ref[...] + lnb_ref[...]
    y_ref[0] = yn * g_s[...] + bg_s[...]


def _rwkv(x3, p, vmix, v_first, *, ts=512):
    b, s, d = x3.shape
    width = p["kkw"].shape[-1]
    has_vmix = vmix is not None
    tile = lambda w: pl.BlockSpec((1, ts, w), lambda i, j: (i, j, 0))
    names = ("gmix", "wr", "mu", "wl", "bl", "g2", "kkw", "ka", "rk", "lnw", "lnb", "ones")
    args = [x3] + [p[n] for n in names]
    in_specs = [tile(d)] + [_const_spec(p[n].shape) for n in names]
    if has_vmix:
        args += list(vmix) + [v_first]
        in_specs += [_const_spec(a.shape) for a in vmix] + [tile(width)]
        out_shape = jax.ShapeDtypeStruct((b, s, width), F32)
        out_specs = tile(width)
    else:
        out_shape = (jax.ShapeDtypeStruct((b, s, width), F32),) * 2
        out_specs = (tile(width), tile(width))
    ucols = p["wr"].shape[1]
    tw = pltpu.VMEM((ts, width), F32)
    nc, npairs = ts // CHUNK, width // LANES
    per = lambda rows, dt: pltpu.VMEM((nc, npairs, rows, LANES), dt)
    scratch = ([pltpu.VMEM((8, ucols), F32), pltpu.VMEM((npairs, LANES, LANES), F32)] + [tw] * 9
               + [per(2 * CHUNK, BF16), per(CHUNK, BF16), per(CHUNK, BF16), per(CHUNK, BF16),
                  per(CHUNK, F32), per(CHUNK, F32), per(LANES, F32), pltpu.VMEM((nc, 8, width), F32)])
    return pl.pallas_call(
        functools.partial(_rwkv_kernel, ts=ts, width=width, has_vmix=has_vmix),
        grid=(b, s // ts),
        in_specs=in_specs,
        out_specs=out_specs,
        out_shape=out_shape,
        scratch_shapes=scratch,
        compiler_params=pltpu.CompilerParams(
            dimension_semantics=("arbitrary", "arbitrary"), vmem_limit_bytes=VMEM_LIMIT),
        name="rwkv_vmix" if has_vmix else "rwkv",
    )(*args)


def _gla_kernel(x_ref, gmix_ref, wg_ref, conv_ref, aup_ref, ab_ref, nw_ref, y_ref,
                carry, st, q_s, k_s, v_s, la_s, o_s, *, ts, kw, vw, n_heads):
    dv = vw // n_heads
    qkv_w = 2 * kw + vw

    @pl.when(pl.program_id(1) == 0)
    def _():
        carry[...] = jnp.zeros_like(carry)
        st[...] = jnp.zeros_like(st)

    x = x_ref[0]
    h = _rms(x, gmix_ref[...]).astype(BF16)
    u = _dot(h, wg_ref[...])
    qkv = u[:, :qkv_w]
    go = u[:, qkv_w:qkv_w + vw]
    al = u[:, qkv_w + vw:qkv_w + vw + LANES]
    rows = lax.broadcasted_iota(jnp.int32, (ts, 1), 0)
    prev8 = carry[...]
    taps = conv_ref.shape[0]
    out = qkv * conv_ref[taps - 1:taps, :]
    for j in range(1, taps):
        shifted = pltpu.roll(qkv, j, 0)
        fill = pltpu.roll(prev8, j, 0)
        first = jnp.where(rows[:8] < j, fill, shifted[:8])
        shifted = jnp.concatenate([first, shifted[8:]], axis=0)
        out = out + shifted * conv_ref[taps - 1 - j:taps - j, :]
    carry[...] = qkv[ts - 8:ts, :]
    qkv = out * _sigmoid(out)
    log_a = _mm(al, aup_ref[...], NN, 2, 1) + ab_ref[...]
    log_a = (jnp.minimum(log_a, 0.0) - jnp.log(1.0 + jnp.exp(-jnp.abs(log_a)))) * (1.0 / GLA_TAU)
    q_s[...] = qkv[:, :kw] * (float(kw // n_heads) ** -0.5)
    k_s[...] = qkv[:, kw:2 * kw]
    v_s[...] = qkv[:, 2 * kw:]
    la_s[...] = log_a

    lane_lo = lax.broadcasted_iota(jnp.int32, (1, LANES), 1) < HEAD
    r = lax.broadcasted_iota(jnp.int32, (CHUNK, CHUNK), 0)
    c = lax.broadcasted_iota(jnp.int32, (CHUNK, CHUNK), 1)
    lower = r >= c
    tri = lower.astype(BF16)

    def chunk(ci, cr):
        off = pl.multiple_of(ci * CHUNK, CHUNK)
        sl = pl.ds(off, CHUNK)
        gc = _mm(tri, la_s[sl, :], NN, 1, 3)
        e_g = jnp.exp(gc)
        e_ng = jnp.exp(-gc)
        e_last = jnp.exp(gc[CHUNK - 1:CHUNK, :])
        q = q_s[sl, :]
        k = k_s[sl, :]
        qg, qr = q * e_g, q * e_ng
        kg, kr = k * e_ng, k * e_g
        kbar = kg * e_last
        vc = v_s[sl, :]
        for pr in range(kw // LANES):
            ls = slice(pr * LANES, (pr + 1) * LANES)
            a_past = _mm(_stack_heads(qg[:, ls], lane_lo), kg[:, ls], NT)
            a_fut = _mm(_stack_heads(qr[:, ls], lane_lo), kr[:, ls], NT)
            for hh in range(2):
                hd = 2 * pr + hh
                a = jnp.where(lower, a_past[hh * CHUNK:(hh + 1) * CHUNK], a_fut[hh * CHUNK:(hh + 1) * CHUNK])
                vh = vc[:, hd * dv:(hd + 1) * dv]
                s_h = st[hd]
                o = _mm(a, vh) + _mm(qg[:, ls], s_h, NT)
                own = lane_lo if hh == 0 else jnp.logical_not(lane_lo)
                st[hd] = s_h * e_last[:, ls] + _mm(vh, jnp.where(own, kbar[:, ls], 0.0), TN)
                o_s[sl, hd * dv:(hd + 1) * dv] = o
        return cr

    lax.fori_loop(0, ts // CHUNK, chunk, 0)

    for hd in range(n_heads):
        hs = slice(hd * dv, (hd + 1) * dv)
        o = o_s[:, hs]
        o = o * lax.rsqrt(jnp.mean(o * o, axis=-1, keepdims=True) + NORM_EPS) * nw_ref[:, hs]
        gate = go[:, hs]
        y_ref[0, :, hs] = o * (gate * _sigmoid(gate))


def _gla(x3, p, *, ts=512):
    b, s, d = x3.shape
    kw = p["aup"].shape[1]
    vw = p["nw"].shape[1]
    n_heads = kw // HEAD
    tile = lambda w: pl.BlockSpec((1, ts, w), lambda i, j: (i, j, 0))
    names = ("gmix", "wg", "conv", "aup", "ab", "nw")
    tkw = pltpu.VMEM((ts, kw), F32)
    tvw = pltpu.VMEM((ts, vw), F32)
    return pl.pallas_call(
        functools.partial(_gla_kernel, ts=ts, kw=kw, vw=vw, n_heads=n_heads),
        grid=(b, s // ts),
        in_specs=[tile(d)] + [_const_spec(p[n].shape) for n in names],
        out_specs=tile(vw),
        out_shape=jax.ShapeDtypeStruct((b, s, vw), F32),
        scratch_shapes=[pltpu.VMEM((8, 2 * kw + vw), F32), pltpu.VMEM((n_heads, vw // n_heads, LANES), F32),
                        tkw, tkw, tvw, tkw, tvw],
        compiler_params=pltpu.CompilerParams(
            dimension_semantics=("arbitrary", "arbitrary"), vmem_limit_bytes=VMEM_LIMIT),
        name="gla",
    )(x3, *[p[n] for n in names])


def _xa_kv_kernel(mem_ref, g_ref, wkv_ref, k_ref, v_ref, *, width):
    m = _rms(mem_ref[0], g_ref[...]).astype(BF16)
    kv = _dot(m, wkv_ref[...])
    k_ref[0] = kv[:, :width].astype(BF16)
    v_ref[0] = kv[:, width:].astype(BF16)


def _xa_kv(mem, g, wkv):
    b, m, d = mem.shape
    width = wkv.shape[1] // 2
    blk = lambda w: pl.BlockSpec((1, m, w), lambda i: (i, 0, 0))
    return pl.pallas_call(
        functools.partial(_xa_kv_kernel, width=width),
        grid=(b,),
        in_specs=[blk(d), _const_spec(g.shape), _const_spec(wkv.shape)],
        out_specs=(blk(width), blk(width)),
        out_shape=(jax.ShapeDtypeStruct((b, m, width), BF16),) * 2,
        compiler_params=pltpu.CompilerParams(dimension_semantics=("arbitrary",), vmem_limit_bytes=VMEM_LIMIT),
        name="xa_kv",
    )(mem, g, wkv)


def _xa_kernel(x_ref, gmix_ref, wq_ref, k_ref, v_ref, y_ref, *, n_heads, hd):
    h = _rms(x_ref[0], gmix_ref[...]).astype(BF16)
    q = _dot(h, wq_ref[...]) * (float(hd) ** -0.5)
    for i in range(n_heads):
        hs = slice(i * hd, (i + 1) * hd)
        sc = _dot(q[:, hs].astype(BF16), k_ref[0, :, hs], NT)
        sc = sc - jnp.max(sc, axis=-1, keepdims=True)
        e = jnp.exp(sc)
        pr = e / jnp.sum(e, axis=-1, keepdims=True)
        y_ref[0, :, hs] = _dot(pr.astype(BF16), v_ref[0, :, hs])


def _xa(x3, gmix, wq, k, v, *, ts=512, hd=128):
    b, s, d = x3.shape
    m, width = k.shape[1], k.shape[2]
    tile = lambda w: pl.BlockSpec((1, ts, w), lambda i, j: (i, j, 0))
    kvs = pl.BlockSpec((1, m, width), lambda i, j: (i, 0, 0))
    return pl.pallas_call(
        functools.partial(_xa_kernel, n_heads=width // hd, hd=hd),
        grid=(b, s // ts),
        in_specs=[tile(d), _const_spec(gmix.shape), _const_spec(wq.shape), kvs, kvs],
        out_specs=tile(width),
        out_shape=jax.ShapeDtypeStruct((b, s, width), F32),
        compiler_params=pltpu.CompilerParams(
            dimension_semantics=("arbitrary", "arbitrary"), vmem_limit_bytes=VMEM_LIMIT),
        name="xa",
    )(x3, gmix, wq, k, v)


def _merge_kernel(x_ref, yr_ref, yg_ref, yx_ref, gmix_ref, wgate_ref, wb_ref, wo_ref, o_ref):
    x = x_ref[...]
    d = x.shape[-1]
    h = _rms(x, gmix_ref[...]).astype(BF16)
    merged = None
    for j, y_ref in enumerate((yr_ref, yg_ref, yx_ref)):
        gate = _sigmoid(_dot(h, wgate_ref[:, j * d:(j + 1) * d]))
        t = _dot(y_ref[...].astype(BF16), wb_ref[j]) * gate
        merged = t if merged is None else merged + t
    o_ref[...] = x + _dot(merged.astype(BF16), wo_ref[...])


def _merge(x, yr, yg, yx, gmix, wgate, wb, wo, *, tm=512):
    t, d = x.shape
    bw = yr.shape[1]
    row = lambda w: pl.BlockSpec((tm, w), lambda i: (i, 0))
    return pl.pallas_call(
        _merge_kernel,
        grid=(t // tm,),
        in_specs=[row(d), row(bw), row(bw), row(bw), _const_spec(gmix.shape), _const_spec(wgate.shape),
                  _const_spec(wb.shape), _const_spec(wo.shape)],
        out_specs=row(d),
        out_shape=jax.ShapeDtypeStruct((t, d), F32),
        compiler_params=pltpu.CompilerParams(dimension_semantics=("arbitrary",), vmem_limit_bytes=VMEM_LIMIT),
        name="merge",
    )(x, yr, yg, yx, gmix, wgate, wb, wo)


def _block_ones(width, block):
    i = jnp.arange(width) // block
    return (i[:, None] == i[None, :]).astype(BF16)


def kernel(x, mem, ffn1_norm, ffn1_w_in, ffn1_w_out, mix_norm, mem_norm, w_in, rwkv_mu, rwkv_w0, rwkv_w2, rwkv_a0, rwkv_a2, rwkv_g2, rwkv_k_k, rwkv_k_a, rwkv_r_k, rwkv_ln_w, rwkv_ln_b, rwkv_v0, rwkv_v1, rwkv_v2, gla_conv, gla_a_up, gla_a_bias, gla_norm, xa_w_kv, w_branch, w_out, ffn2_norm, ffn2_w_in, ffn2_w_out, final_norm):
    b, s, d = x.shape
    depth = w_in.shape[0]
    rw = rwkv_w0.shape[1]
    dl, al_ = rwkv_w2.shape[1], rwkv_a2.shape[1]
    gl_ = rwkv_g2.shape[1]
    kw = gla_a_up.shape[2]
    vw = gla_norm.shape[1]
    glora = gla_a_up.shape[1]
    xw = xa_w_kv.shape[2] // 2
    rcols = 3 * rw + dl + al_ + gl_
    gcols = 2 * kw + vw + glora + vw
    assert dl + al_ == LANES and gl_ == LANES and dl == HEAD

    row1 = lambda a: a[:, None, :]
    w_r = w_in[:, :, :rcols].astype(BF16)
    g0 = rcols
    w_gqkv = w_in[:, :, g0:g0 + 2 * kw + vw]
    w_gal = w_in[:, :, g0 + 2 * kw + vw:g0 + 2 * kw + vw + glora]
    w_ggo = w_in[:, :, g0 + 2 * kw + vw + glora:g0 + gcols]
    w_g = jnp.concatenate([w_gqkv, w_ggo, jnp.pad(w_gal, ((0, 0), (0, 0), (0, LANES - glora)))], axis=-1).astype(BF16)
    w_xq = w_in[:, :, g0 + gcols:g0 + gcols + xw].astype(BF16)
    w_gate = w_in[:, :, g0 + gcols + xw:].astype(BF16)
    zl = jnp.zeros((depth, dl, rw), F32)
    w_lora = jnp.concatenate([jnp.concatenate([rwkv_w2, zl], axis=-1),
                              jnp.concatenate([zl, rwkv_a2], axis=-1)], axis=1).astype(BF16)
    b_lora = row1(jnp.concatenate([rwkv_w0, rwkv_a0], axis=-1))
    mv = rwkv_v1.shape[2]
    v1p = jnp.pad(rwkv_v1, ((0, 0), (0, 0), (0, LANES - mv))).astype(BF16)
    v2p = jnp.pad(rwkv_v2, ((0, 0), (0, LANES - mv), (0, 0))).astype(BF16)
    aup = jnp.pad(gla_a_up, ((0, 0), (0, LANES - glora), (0, 0))).astype(BF16)
    ones = _block_ones(rw, HEAD)
    f1i, f1o = ffn1_w_in.astype(BF16), ffn1_w_out.astype(BF16)
    f2i, f2o = ffn2_w_in.astype(BF16), ffn2_w_out.astype(BF16)
    wkv = xa_w_kv.astype(BF16)
    wb = w_branch.astype(BF16)
    wo = w_out.astype(BF16)
    g2 = rwkv_g2.astype(BF16)

    xf = x.reshape(b * s, d)
    v_first = None
    for l in range(depth):
        gmix = mix_norm[l][None, :]
        xf = _ffn(xf, ffn1_norm[l][None, :], f1i[l], f1o[l])
        x3 = xf.reshape(b, s, d)
        rp = dict(gmix=gmix, wr=w_r[l], mu=rwkv_mu[l][None, :], wl=w_lora[l], bl=b_lora[l], g2=g2[l],
                  kkw=rwkv_k_k[l][None, :], ka=rwkv_k_a[l][None, :], rk=rwkv_r_k[l].reshape(1, rw),
                  lnw=rwkv_ln_w[l][None, :], lnb=rwkv_ln_b[l][None, :], ones=ones)
        if l == 0:
            y_r, v_first = _rwkv(x3, rp, None, None)
        else:
            y_r = _rwkv(x3, rp, (rwkv_v0[l - 1][None, :], v1p[l - 1], v2p[l - 1]), v_first)
        gp = dict(gmix=gmix, wg=w_g[l], conv=gla_conv[l], aup=aup[l], ab=gla_a_bias[l][None, :],
                  nw=gla_norm[l][None, :])
        y_g = _gla(x3, gp)
        k_m, v_m = _xa_kv(mem, mem_norm[l][None, :], wkv[l])
        y_x = _xa(x3, gmix, w_xq[l], k_m, v_m)
        xf = _merge(xf, y_r.reshape(b * s, rw), y_g.reshape(b * s, vw), y_x.reshape(b * s, xw),
                    gmix, w_gate[l], wb[l], wo[l])
        fin = final_norm[None, :] if l == depth - 1 else None
        xf = _ffn(xf, ffn2_norm[l][None, :], f2i[l], f2o[l], fin)
    return xf.reshape(b, s, d)
```

```python
import functools

import jax
import jax.numpy as jnp
from jax import lax
from jax.experimental import pallas as pl
from jax.experimental.pallas import tpu as pltpu

F32 = jnp.float32
BF16 = jnp.bfloat16

NORM_EPS = 1e-6
RWKV_LN_EPS = 64e-5
GLA_TAU = 16.0
CHUNK = 64
HEAD = 64
LANES = 128
GROUP = 4
VMEM_LIMIT = 56 * 1024 * 1024

NN = ((1,), (0,))
NT = ((1,), (1,))
TN = ((0,), (0,))


def _dot(a, b, dims=NN):
    return lax.dot_general(a, b, (dims, ((), ())), preferred_element_type=F32)


def _split(x, n):
    if x.dtype == BF16:
        return [x]
    parts, rest = [], x
    for i in range(n):
        p = rest.astype(BF16)
        parts.append(p)
        if i + 1 < n:
            rest = rest - p.astype(F32)
    return parts


def _mm(a, b, dims=NN, pa=1, pb=1):
    aa, bb = _split(a, pa), _split(b, pb)
    order = max(len(aa), len(bb))
    out = None
    for i, x in enumerate(aa):
        for j, y in enumerate(bb):
            if i + j < order:
                t = _dot(x, y, dims)
                out = t if out is None else out + t
    return out


def _sigmoid(x):
    return 1.0 / (1.0 + jnp.exp(-x))


def _softplus(x):
    return jnp.maximum(x, 0.0) + jnp.log(1.0 + jnp.exp(-jnp.abs(x)))


def _rms(x, g):
    return x * lax.rsqrt(jnp.mean(x * x, axis=-1, keepdims=True) + NORM_EPS) * g


def _stack_heads(x, lane_lo):
    x = x.astype(BF16)
    zero = jnp.zeros_like(x)
    return jnp.concatenate([jnp.where(lane_lo, x, zero), jnp.where(lane_lo, zero, x)], axis=0)


def _segsum(x, ones):
    w = ones.shape[0]
    xb = x.astype(BF16)
    return jnp.concatenate([_dot(xb[:, i:i + w], ones) for i in range(0, x.shape[1], w)], axis=1)


def _const_spec(shape):
    nd = len(shape)
    return pl.BlockSpec(shape, lambda *_: (0,) * nd)


def _ffn_kernel(x_ref, g_ref, win_ref, wout_ref, *rest, d_ff, tf, final):
    if final:
        gf_ref, o_ref, acc_ref = rest
    else:
        o_ref, acc_ref = rest
    x = x_ref[...]
    h = _rms(x, g_ref[...]).astype(BF16)
    for c in range(d_ff // tf):
        gate = _dot(h, win_ref[:, c * tf:(c + 1) * tf])
        up = _dot(h, win_ref[:, d_ff + c * tf:d_ff + (c + 1) * tf])
        act = (gate * _sigmoid(gate) * up).astype(BF16)
        part = _dot(act, wout_ref[c * tf:(c + 1) * tf, :])
        if c == 0:
            acc_ref[...] = part
        else:
            acc_ref[...] += part
    y = x + 0.5 * acc_ref[...]
    if final:
        y = _rms(y, gf_ref[...])
    o_ref[...] = y


def _ffn(x, g, w_in, w_out, final_g=None, *, tm=512, tf=256):
    t, d = x.shape
    d_ff = w_out.shape[0]
    final = final_g is not None
    row = pl.BlockSpec((tm, d), lambda i: (i, 0))
    in_specs = [row, _const_spec((1, d)), _const_spec(w_in.shape), _const_spec(w_out.shape)]
    args = [x, g, w_in, w_out]
    if final:
        in_specs.append(_const_spec((1, d)))
        args.append(final_g)
    return pl.pallas_call(
        functools.partial(_ffn_kernel, d_ff=d_ff, tf=tf, final=final),
        grid=(t // tm,),
        in_specs=in_specs,
        out_specs=row,
        out_shape=jax.ShapeDtypeStruct((t, d), F32),
        scratch_shapes=[pltpu.VMEM((tm, d), F32)],
        compiler_params=pltpu.CompilerParams(
            dimension_semantics=("arbitrary",), vmem_limit_bytes=VMEM_LIMIT),
        name="ffn_final" if final else "ffn",
    )(*args)


def _rwkv_phase1(probs, masks):
    lane_lo, strict, incl, levels, eye, blockdiag = masks
    n = range(len(probs))
    ar = [jnp.concatenate([q["at"], q["rt"]], axis=0).astype(BF16) for q in probs]
    ab = [_dot(ar[i], _stack_heads(probs[i]["bh"], lane_lo), NT) for i in n]
    yield
    ak = [_dot(ar[i], _stack_heads(probs[i]["kh"], lane_lo), NT) for i in n]
    a_ab = [jnp.where(strict, ab[i][:CHUNK], 0.0) for i in n]
    a_rb = [jnp.where(incl, ab[i][CHUNK:], 0.0).astype(BF16) for i in n]
    t = [eye + jnp.where(levels[0], a_ab[i], 0.0) for i in n]
    yield
    for m in levels[1:]:
        x = [_dot(t[i].astype(BF16), _stack_heads(jnp.where(m, a_ab[i], 0.0), lane_lo)) for i in n]
        yield
        t = [t[i] + _dot(x[i].astype(BF16), _stack_heads(t[i], lane_lo)) for i in n]
        yield
    a_ak = [jnp.where(strict, ak[i][:CHUNK], 0.0).astype(BF16) for i in n]
    a_rk = [jnp.where(incl, ak[i][CHUNK:], 0.0).astype(BF16) for i in n]
    vst = [_stack_heads(q["v"], lane_lo) for q in probs]
    av = [_dot(a_ak[i], vst[i]) for i in n]
    arkv = [_dot(a_rk[i], vst[i]) for i in n]
    yield
    vk = [jnp.where(blockdiag, _dot(probs[i]["v"].astype(BF16), probs[i]["kbar"].astype(BF16), TN), 0.0)
          for i in n]
    for i, q in enumerate(probs):
        q["store"](ar=ar[i], t=t[i].astype(BF16), arb=a_rb[i], bbar=q["bbar"].astype(BF16),
                   av=av[i], arkv=arkv[i], vk=vk[i])
    yield


def _interleave(*gens):
    live = list(gens)
    while live:
        for g in list(live):
            try:
                next(g)
            except StopIteration:
                live.remove(g)


def _pair_masks():
    row = lax.broadcasted_iota(jnp.int32, (CHUNK, LANES), 0)
    col = lax.broadcasted_iota(jnp.int32, (CHUNK, LANES), 1) % HEAD
    lane_lo = lax.broadcasted_iota(jnp.int32, (1, LANES), 1) < HEAD
    strict = row > col
    incl = row >= col
    levels = []
    for k in range(6):
        levels.append(((row >> (k + 1)) == (col >> (k + 1)))
                      & (((row >> k) & 1) == 1) & (((col >> k) & 1) == 0))
    eye = (row == col).astype(F32)
    r2 = lax.broadcasted_iota(jnp.int32, (LANES, LANES), 0) < HEAD
    c2 = lax.broadcasted_iota(jnp.int32, (LANES, LANES), 1) < HEAD
    blockdiag = r2 == c2
    return lane_lo, strict, incl, levels, eye, blockdiag


def _tri_ones():
    r = lax.broadcasted_iota(jnp.int32, (CHUNK, CHUNK), 0)
    c = lax.broadcasted_iota(jnp.int32, (CHUNK, CHUNK), 1)
    return (r >= c).astype(BF16)


def _rwkv_kernel(*refs, ts, width, has_vmix):
    (x_ref, gmix_ref, wr_ref, mu_ref, wl_ref, bl_ref, g2_ref, kkw_ref, ka_ref, rk_ref,
     lnw_ref, lnb_ref, ones_ref) = refs[:13]
    refs = refs[13:]
    if has_vmix:
        v0_ref, v1_ref, v2_ref, vf_ref = refs[:4]
        refs = refs[4:]
        y_ref = refs[0]
        refs = refs[1:]
    else:
        y_ref, vf_out_ref = refs[:2]
        refs = refs[2:]
    (uprev, st, r_s, k_s, v_s, kk_s, b_s, lw_s, g_s, bg_s, y_s,
     ar_s, t_s, arb_s, bbar_s, av_s, arkv_s, vk_s, egc_s) = refs
    n_pairs = width // LANES

    @pl.when(pl.program_id(1) == 0)
    def _():
        uprev[...] = jnp.zeros_like(uprev)
        st[...] = jnp.zeros_like(st)

    x = x_ref[0]
    h = _rms(x, gmix_ref[...]).astype(BF16)
    u = _dot(h, wr_ref[...])
    rows = lax.broadcasted_iota(jnp.int32, (ts, 1), 0)
    u_prev = jnp.where(rows == 0, uprev[0:1, :], pltpu.roll(u, 1, 0))
    uprev[...] = pltpu.roll(u[ts - 8:ts, :], 1, 0)
    us = u + mu_ref[...] * (u_prev - u)
    r = us[:, 0:width]
    k = us[:, width:2 * width]
    v = us[:, 2 * width:3 * width]
    wa = us[:, 3 * width:3 * width + LANES]
    gl = us[:, 3 * width + LANES:3 * width + 2 * LANES]
    if has_vmix:
        z = _mm(_mm(v, v1_ref[...], NN, 2, 1), v2_ref[...], NN, 2, 1)
        v = v + (vf_ref[0] - v) * _sigmoid(v0_ref[...] + z)
    else:
        vf_out_ref[0] = v
    lane = lax.broadcasted_iota(jnp.int32, (1, LANES), 1)
    act = jnp.where(lane < HEAD, jnp.tanh(wa), wa)
    lo = _mm(act, wl_ref[...], NN, 2, 1) + bl_ref[...]
    w_log = -_softplus(-lo[:, :width]) - 0.5
    lw = -jnp.exp(w_log)
    a = _sigmoid(lo[:, width:])
    g = _mm(_sigmoid(gl), g2_ref[...], NN, 2, 1)
    ones = ones_ref[...]
    kk = k * kkw_ref[...]
    ss = _segsum(kk * kk, ones)
    kk = kk / jnp.maximum(jnp.sqrt(ss), 1e-12)
    kh = k * (1.0 + (a - 1.0) * ka_ref[...])
    bonus = _segsum(r * kh * rk_ref[...], ones) * v
    r_s[...] = r
    k_s[...] = kh
    v_s[...] = v
    kk_s[...] = kk
    b_s[...] = kk * a
    lw_s[...] = lw
    g_s[...] = g
    bg_s[...] = bonus * g

    masks = _pair_masks()
    tri = _tri_ones()

    lane_lo, blockdiag = masks[0], masks[5]
    n_groups = ts // (CHUNK * GROUP)
    inv_n = 1.0 / HEAD

    def store_to(c, p):
        def store(ar, t, arb, bbar, av, arkv, vk):
            ar_s[c, p] = ar
            t_s[c, p] = t
            arb_s[c, p] = arb
            bbar_s[c, p] = bbar
            av_s[c, p] = av
            arkv_s[c, p] = arkv
            vk_s[c, p] = vk
        return store

    def phase1(gi):
        probs = []
        for j in range(GROUP):
            c = gi * GROUP + j
            sl = slice(c * CHUNK, (c + 1) * CHUNK)
            lwc = lw_s[sl, :]
            gc = _mm(tri, lwc, NN, 1, 3)
            e_g = jnp.exp(gc)
            e_ng = jnp.exp(-gc)
            e_gm = jnp.exp(gc - lwc)
            e_gc = jnp.exp(gc[CHUNK - 1:CHUNK, :])
            egc_s[c] = jnp.broadcast_to(e_gc, (8, width))
            rt = r_s[sl, :] * e_g
            at = -kk_s[sl, :] * e_gm
            bh = b_s[sl, :] * e_ng
            kh_ = k_s[sl, :] * e_ng
            bbar = bh * e_gc
            kbar = kh_ * e_gc
            vc = v_s[sl, :]
            for p in range(n_pairs):
                ls = slice(p * LANES, (p + 1) * LANES)
                probs.append(dict(rt=rt[:, ls], at=at[:, ls], bh=bh[:, ls], kh=kh_[:, ls],
                                  bbar=bbar[:, ls], kbar=kbar[:, ls], v=vc[:, ls], store=store_to(c, p)))
        yield from _rwkv_phase1(probs, masks)

    def phase2(gi):
        pairs = range(n_pairs)
        for c in range(gi * GROUP, (gi + 1) * GROUP):
            sl = slice(c * CHUNK, (c + 1) * CHUNK)
            s = [st[p] for p in pairs]
            wy = [_dot(ar_s[c, p], s[p].astype(BF16), NT) for p in pairs]
            yield
            u = [_dot(t_s[c, p], _stack_heads(wy[p][:CHUNK] + av_s[c, p], lane_lo)) for p in pairs]
            yield
            y = [wy[p][CHUNK:] + _dot(arb_s[c, p], _stack_heads(u[p], lane_lo)) + arkv_s[c, p] for p in pairs]
            e_gc = egc_s[c][0:1, :]
            for p in pairs:
                ls = slice(p * LANES, (p + 1) * LANES)
                upd = _dot(u[p].astype(BF16), bbar_s[c, p], TN)
                st[p] = s[p] * e_gc[:, ls] + jnp.where(blockdiag, upd, 0.0) + vk_s[c, p]
                y_s[sl, ls] = y[p]
            yield

    def post(gi):
        sl = slice(gi * GROUP * CHUNK, (gi + 1) * GROUP * CHUNK)
        y = y_s[sl, :]
        mean = _segsum(y, ones) * inv_n
        yield
        d = y - mean
        var = _segsum(d * d, ones) * inv_n
        yn = d * lax.rsqrt(var + RWKV_LN_EPS) * lnw_ref[...] + lnb_ref[...]
        y_ref[0, sl, :] = yn * g_s[sl, :] + bg_s[sl, :]
        yield

    for step in range(n_groups + 2):
        gens = []
        if step < n_groups:
            gens.append(phase1(step))
        if 0 <= step - 1 < n_groups:
            gens.append(phase2(step - 1))
        if 0 <= step - 2 < n_groups:
            gens.append(post(step - 2))
        _interleave(*gens)


def _rwkv(x3, p, vmix, v_first, *, ts=512):
    b, s, d = x3.shape
    width = p["kkw"].shape[-1]
    has_vmix = vmix is not None
    tile = lambda w: pl.BlockSpec((1, ts, w), lambda i, j: (i, j, 0))
    names = ("gmix", "wr", "mu", "wl", "bl", "g2", "kkw", "ka", "rk", "lnw", "lnb", "ones")
    args = [x3] + [p[n] for n in names]
    in_specs = [tile(d)] + [_const_spec(p[n].shape) for n in names]
    if has_vmix:
        args += list(vmix) + [v_first]
        in_specs += [_const_spec(a.shape) for a in vmix] + [tile(width)]
        out_shape = jax.ShapeDtypeStruct((b, s, width), F32)
        out_specs = tile(width)
    else:
        out_shape = (jax.ShapeDtypeStruct((b, s, width), F32),) * 2
        out_specs = (tile(width), tile(width))
    ucols = p["wr"].shape[1]
    tw = pltpu.VMEM((ts, width), F32)
    nc, npairs = ts // CHUNK, width // LANES
    per = lambda rows, dt: pltpu.VMEM((nc, npairs, rows, LANES), dt)
    scratch = ([pltpu.VMEM((8, ucols), F32), pltpu.VMEM((npairs, LANES, LANES), F32)] + [tw] * 9
               + [per(2 * CHUNK, BF16), per(CHUNK, BF16), per(CHUNK, BF16), per(CHUNK, BF16),
                  per(CHUNK, F32), per(CHUNK, F32), per(LANES, F32), pltpu.VMEM((nc, 8, width), F32)])
    return pl.pallas_call(
        functools.partial(_rwkv_kernel, ts=ts, width=width, has_vmix=has_vmix),
        grid=(b, s // ts),
        in_specs=in_specs,
        out_specs=out_specs,
        out_shape=out_shape,
        scratch_shapes=scratch,
        compiler_params=pltpu.CompilerParams(
            dimension_semantics=("arbitrary", "arbitrary"), vmem_limit_bytes=VMEM_LIMIT),
        name="rwkv_vmix" if has_vmix else "rwkv",
    )(*args)


def _gla_kernel(x_ref, gmix_ref, wg_ref, conv_ref, aup_ref, ab_ref, nw_ref, y_ref,
                carry, st, q_s, k_s, v_s, la_s, o_s, go_s, qg_s, kv_s, el_s, *, ts, kw, vw, n_heads):
    dv = vw // n_heads
    qkv_w = 2 * kw + vw

    @pl.when(pl.program_id(1) == 0)
    def _():
        carry[...] = jnp.zeros_like(carry)
        st[...] = jnp.zeros_like(st)

    x = x_ref[0]
    h = _rms(x, gmix_ref[...]).astype(BF16)
    u = _dot(h, wg_ref[...])
    qkv = u[:, :qkv_w]
    go = u[:, qkv_w:qkv_w + vw]
    al = u[:, qkv_w + vw:qkv_w + vw + LANES]
    rows = lax.broadcasted_iota(jnp.int32, (ts, 1), 0)
    prev8 = carry[...]
    taps = conv_ref.shape[0]
    out = qkv * conv_ref[taps - 1:taps, :]
    for j in range(1, taps):
        shifted = pltpu.roll(qkv, j, 0)
        fill = pltpu.roll(prev8, j, 0)
        first = jnp.where(rows[:8] < j, fill, shifted[:8])
        shifted = jnp.concatenate([first, shifted[8:]], axis=0)
        out = out + shifted * conv_ref[taps - 1 - j:taps - j, :]
    carry[...] = qkv[ts - 8:ts, :]
    qkv = out * _sigmoid(out)
    log_a = _mm(al, aup_ref[...], NN, 2, 1) + ab_ref[...]
    log_a = (jnp.minimum(log_a, 0.0) - jnp.log(1.0 + jnp.exp(-jnp.abs(log_a)))) * (1.0 / GLA_TAU)
    q_s[...] = qkv[:, :kw] * (float(kw // n_heads) ** -0.5)
    k_s[...] = qkv[:, kw:2 * kw]
    v_s[...] = qkv[:, 2 * kw:].astype(BF16)
    la_s[...] = log_a
    go_s[...] = go * _sigmoid(go)

    lane_lo = lax.broadcasted_iota(jnp.int32, (1, LANES), 1) < HEAD
    r = lax.broadcasted_iota(jnp.int32, (CHUNK, CHUNK), 0)
    c = lax.broadcasted_iota(jnp.int32, (CHUNK, CHUNK), 1)
    lower = r >= c
    tri = lower.astype(BF16)
    n_groups = ts // (CHUNK * GROUP)
    n_kpairs = kw // LANES

    def phase1(gi):
        chunks = range(gi * GROUP, (gi + 1) * GROUP)
        pre = {}
        for ci in chunks:
            sl = slice(ci * CHUNK, (ci + 1) * CHUNK)
            gc = _mm(tri, la_s[sl, :], NN, 1, 3)
            e_g = jnp.exp(gc)
            e_ng = jnp.exp(-gc)
            e_last = jnp.exp(gc[CHUNK - 1:CHUNK, :])
            el_s[ci] = jnp.broadcast_to(e_last, (8, kw))
            q = q_s[sl, :]
            k = k_s[sl, :]
            qg = q * e_g
            kg = k * e_ng
            qg_s[ci] = qg.astype(BF16)
            pre[ci] = (qg, q * e_ng, kg.astype(BF16), (k * e_g).astype(BF16), (kg * e_last).astype(BF16))
        yield
        a_past = {(ci, pr): _dot(_stack_heads(pre[ci][0][:, pr * LANES:(pr + 1) * LANES], lane_lo),
                                 pre[ci][2][:, pr * LANES:(pr + 1) * LANES], NT)
                  for ci in chunks for pr in range(n_kpairs)}
        yield
        a_fut = {(ci, pr): _dot(_stack_heads(pre[ci][1][:, pr * LANES:(pr + 1) * LANES], lane_lo),
                                pre[ci][3][:, pr * LANES:(pr + 1) * LANES], NT)
                 for ci in chunks for pr in range(n_kpairs)}
        yield
        zero = jnp.zeros((CHUNK, LANES), BF16)
        for ci in chunks:
            sl = slice(ci * CHUNK, (ci + 1) * CHUNK)
            for hd in range(n_heads):
                pr, hh = divmod(hd, 2)
                hs = slice(hd * dv, (hd + 1) * dv)
                rs = slice(hh * CHUNK, (hh + 1) * CHUNK)
                a = jnp.where(lower, a_past[ci, pr][rs], a_fut[ci, pr][rs]).astype(BF16)
                vh = v_s[sl, hs]
                o_s[sl, hs] = _dot(a, vh)
                kb = pre[ci][4][:, pr * LANES:(pr + 1) * LANES]
                kb = jnp.where(lane_lo, kb, zero) if hh == 0 else jnp.where(lane_lo, zero, kb)
                kv_s[ci, hd] = _dot(vh, kb, TN)
            yield

    def phase2(gi):
        for ci in range(gi * GROUP, (gi + 1) * GROUP):
            sl = slice(ci * CHUNK, (ci + 1) * CHUNK)
            e_last = el_s[ci][0:1, :]
            for hd in range(n_heads):
                pr = hd // 2
                ls = slice(pr * LANES, (pr + 1) * LANES)
                hs = slice(hd * dv, (hd + 1) * dv)
                s_h = st[hd]
                o_s[sl, hs] += _dot(qg_s[ci][:, ls], s_h.astype(BF16), NT)
                st[hd] = s_h * e_last[:, ls] + kv_s[ci, hd]
            yield

    def post(gi):
        sl = slice(gi * GROUP * CHUNK, (gi + 1) * GROUP * CHUNK)
        for hd in range(n_heads):
            hs = slice(hd * dv, (hd + 1) * dv)
            o = o_s[sl, hs]
            o = o * lax.rsqrt(jnp.mean(o * o, axis=-1, keepdims=True) + NORM_EPS) * nw_ref[:, hs]
            y_ref[0, sl, hs] = o * go_s[sl, hs]
            yield

    for step in range(n_groups + 2):
        gens = []
        if step < n_groups:
            gens.append(phase1(step))
        if 0 <= step - 1 < n_groups:
            gens.append(phase2(step - 1))
        if 0 <= step - 2 < n_groups:
            gens.append(post(step - 2))
        _interleave(*gens)


def _gla(x3, p, *, ts=512):
    b, s, d = x3.shape
    kw = p["aup"].shape[1]
    vw = p["nw"].shape[1]
    n_heads = kw // HEAD
    tile = lambda w: pl.BlockSpec((1, ts, w), lambda i, j: (i, j, 0))
    names = ("gmix", "wg", "conv", "aup", "ab", "nw")
    tkw = pltpu.VMEM((ts, kw), F32)
    tvw = pltpu.VMEM((ts, vw), F32)
    return pl.pallas_call(
        functools.partial(_gla_kernel, ts=ts, kw=kw, vw=vw, n_heads=n_heads),
        grid=(b, s // ts),
        in_specs=[tile(d)] + [_const_spec(p[n].shape) for n in names],
        out_specs=tile(vw),
        out_shape=jax.ShapeDtypeStruct((b, s, vw), F32),
        scratch_shapes=[pltpu.VMEM((8, 2 * kw + vw), F32), pltpu.VMEM((n_heads, vw // n_heads, LANES), F32),
                        tkw, tkw, pltpu.VMEM((ts, vw), BF16), tkw, tvw, tvw,
                        pltpu.VMEM((ts // CHUNK, CHUNK, kw), BF16),
                        pltpu.VMEM((ts // CHUNK, n_heads, vw // n_heads, LANES), F32),
                        pltpu.VMEM((ts // CHUNK, 8, kw), F32)],
        compiler_params=pltpu.CompilerParams(
            dimension_semantics=("arbitrary", "arbitrary"), vmem_limit_bytes=VMEM_LIMIT),
        name="gla",
    )(x3, *[p[n] for n in names])


def _xa_kv_kernel(mem_ref, g_ref, wkv_ref, k_ref, v_ref, *, width):
    m = _rms(mem_ref[0], g_ref[...]).astype(BF16)
    kv = _dot(m, wkv_ref[...])
    k_ref[0] = kv[:, :width].astype(BF16)
    v_ref[0] = kv[:, width:].astype(BF16)


def _xa_kv(mem, g, wkv):
    b, m, d = mem.shape
    width = wkv.shape[1] // 2
    blk = lambda w: pl.BlockSpec((1, m, w), lambda i: (i, 0, 0))
    return pl.pallas_call(
        functools.partial(_xa_kv_kernel, width=width),
        grid=(b,),
        in_specs=[blk(d), _const_spec(g.shape), _const_spec(wkv.shape)],
        out_specs=(blk(width), blk(width)),
        out_shape=(jax.ShapeDtypeStruct((b, m, width), BF16),) * 2,
        compiler_params=pltpu.CompilerParams(dimension_semantics=("arbitrary",), vmem_limit_bytes=VMEM_LIMIT),
        name="xa_kv",
    )(mem, g, wkv)


def _xa_kernel(x_ref, gmix_ref, wq_ref, k_ref, v_ref, y_ref, *, n_heads, hd):
    h = _rms(x_ref[0], gmix_ref[...]).astype(BF16)
    q = _dot(h, wq_ref[...]) * (float(hd) ** -0.5)
    for i in range(n_heads):
        hs = slice(i * hd, (i + 1) * hd)
        sc = _dot(q[:, hs].astype(BF16), k_ref[0, :, hs], NT)
        sc = sc - jnp.max(sc, axis=-1, keepdims=True)
        e = jnp.exp(sc)
        pr = e / jnp.sum(e, axis=-1, keepdims=True)
        y_ref[0, :, hs] = _dot(pr.astype(BF16), v_ref[0, :, hs])


def _xa(x3, gmix, wq, k, v, *, ts=512, hd=128):
    b, s, d = x3.shape
    m, width = k.shape[1], k.shape[2]
    tile = lambda w: pl.BlockSpec((1, ts, w), lambda i, j: (i, j, 0))
    kvs = pl.BlockSpec((1, m, width), lambda i, j: (i, 0, 0))
    return pl.pallas_call(
        functools.partial(_xa_kernel, n_heads=width // hd, hd=hd),
        grid=(b, s // ts),
        in_specs=[tile(d), _const_spec(gmix.shape), _const_spec(wq.shape), kvs, kvs],
        out_specs=tile(width),
        out_shape=jax.ShapeDtypeStruct((b, s, width), F32),
        compiler_params=pltpu.CompilerParams(
            dimension_semantics=("arbitrary", "arbitrary"), vmem_limit_bytes=VMEM_LIMIT),
        name="xa",
    )(x3, gmix, wq, k, v)


def _merge_kernel(x_ref, yr_ref, yg_ref, yx_ref, gmix_ref, wgate_ref, wb_ref, wo_ref, o_ref):
    x = x_ref[...]
    d = x.shape[-1]
    h = _rms(x, gmix_ref[...]).astype(BF16)
    merged = None
    for j, y_ref in enumerate((yr_ref, yg_ref, yx_ref)):
        gate = _sigmoid(_dot(h, wgate_ref[:, j * d:(j + 1) * d]))
        t = _dot(y_ref[...].astype(BF16), wb_ref[j]) * gate
        merged = t if merged is None else merged + t
    o_ref[...] = x + _dot(merged.astype(BF16), wo_ref[...])


def _merge(x, yr, yg, yx, gmix, wgate, wb, wo, *, tm=512):
    t, d = x.shape
    bw = yr.shape[1]
    row = lambda w: pl.BlockSpec((tm, w), lambda i: (i, 0))
    return pl.pallas_call(
        _merge_kernel,
        grid=(t // tm,),
        in_specs=[row(d), row(bw), row(bw), row(bw), _const_spec(gmix.shape), _const_spec(wgate.shape),
                  _const_spec(wb.shape), _const_spec(wo.shape)],
        out_specs=row(d),
        out_shape=jax.ShapeDtypeStruct((t, d), F32),
        compiler_params=pltpu.CompilerParams(dimension_semantics=("arbitrary",), vmem_limit_bytes=VMEM_LIMIT),
        name="merge",
    )(x, yr, yg, yx, gmix, wgate, wb, wo)


def _block_ones(width, block):
    i = jnp.arange(width) // block
    return (i[:, None] == i[None, :]).astype(BF16)


def kernel(x, mem, ffn1_norm, ffn1_w_in, ffn1_w_out, mix_norm, mem_norm, w_in, rwkv_mu, rwkv_w0, rwkv_w2, rwkv_a0, rwkv_a2, rwkv_g2, rwkv_k_k, rwkv_k_a, rwkv_r_k, rwkv_ln_w, rwkv_ln_b, rwkv_v0, rwkv_v1, rwkv_v2, gla_conv, gla_a_up, gla_a_bias, gla_norm, xa_w_kv, w_branch, w_out, ffn2_norm, ffn2_w_in, ffn2_w_out, final_norm):
    b, s, d = x.shape
    depth = w_in.shape[0]
    rw = rwkv_w0.shape[1]
    dl, al_ = rwkv_w2.shape[1], rwkv_a2.shape[1]
    gl_ = rwkv_g2.shape[1]
    kw = gla_a_up.shape[2]
    vw = gla_norm.shape[1]
    glora = gla_a_up.shape[1]
    xw = xa_w_kv.shape[2] // 2
    rcols = 3 * rw + dl + al_ + gl_
    gcols = 2 * kw + vw + glora + vw
    assert dl + al_ == LANES and gl_ == LANES and dl == HEAD

    row1 = lambda a: a[:, None, :]
    w_r = w_in[:, :, :rcols].astype(BF16)
    g0 = rcols
    w_gqkv = w_in[:, :, g0:g0 + 2 * kw + vw]
    w_gal = w_in[:, :, g0 + 2 * kw + vw:g0 + 2 * kw + vw + glora]
    w_ggo = w_in[:, :, g0 + 2 * kw + vw + glora:g0 + gcols]
    w_g = jnp.concatenate([w_gqkv, w_ggo, jnp.pad(w_gal, ((0, 0), (0, 0), (0, LANES - glora)))], axis=-1).astype(BF16)
    w_xq = w_in[:, :, g0 + gcols:g0 + gcols + xw].astype(BF16)
    w_gate = w_in[:, :, g0 + gcols + xw:].astype(BF16)
    zl = jnp.zeros((depth, dl, rw), F32)
    w_lora = jnp.concatenate([jnp.concatenate([rwkv_w2, zl], axis=-1),
                              jnp.concatenate([zl, rwkv_a2], axis=-1)], axis=1).astype(BF16)
    b_lora = row1(jnp.concatenate([rwkv_w0, rwkv_a0], axis=-1))
    mv = rwkv_v1.shape[2]
    v1p = jnp.pad(rwkv_v1, ((0, 0), (0, 0), (0, LANES - mv))).astype(BF16)
    v2p = jnp.pad(rwkv_v2, ((0, 0), (0, LANES - mv), (0, 0))).astype(BF16)
    aup = jnp.pad(gla_a_up, ((0, 0), (0, LANES - glora), (0, 0))).astype(BF16)
    ones = _block_ones(2 * LANES, HEAD)
    f1i, f1o = ffn1_w_in.astype(BF16), ffn1_w_out.astype(BF16)
    f2i, f2o = ffn2_w_in.astype(BF16), ffn2_w_out.astype(BF16)
    wkv = xa_w_kv.astype(BF16)
    wb = w_branch.astype(BF16)
    wo = w_out.astype(BF16)
    g2 = rwkv_g2.astype(BF16)

    xf = x.reshape(b * s, d)
    v_first = None
    for l in range(depth):
        gmix = mix_norm[l][None, :]
        xf = _ffn(xf, ffn1_norm[l][None, :], f1i[l], f1o[l])
        x3 = xf.reshape(b, s, d)
        rp = dict(gmix=gmix, wr=w_r[l], mu=rwkv_mu[l][None, :], wl=w_lora[l], bl=b_lora[l], g2=g2[l],
                  kkw=rwkv_k_k[l][None, :], ka=rwkv_k_a[l][None, :], rk=rwkv_r_k[l].reshape(1, rw),
                  lnw=rwkv_ln_w[l][None, :], lnb=rwkv_ln_b[l][None, :], ones=ones)
        if l == 0:
            y_r, v_first = _rwkv(x3, rp, None, None)
        else:
            y_r = _rwkv(x3, rp, (rwkv_v0[l - 1][None, :], v1p[l - 1], v2p[l - 1]), v_first)
        gp = dict(gmix=gmix, wg=w_g[l], conv=gla_conv[l], aup=aup[l], ab=gla_a_bias[l][None, :],
                  nw=gla_norm[l][None, :])
        y_g = _gla(x3, gp)
        k_m, v_m = _xa_kv(mem, mem_norm[l][None, :], wkv[l])
        y_x = _xa(x3, gmix, w_xq[l], k_m, v_m)
        xf = _merge(xf, y_r.reshape(b * s, rw), y_g.reshape(b * s, vw), y_x.reshape(b * s, xw),
                    gmix, w_gate[l], wb[l], wo[l])
        fin = final_norm[None, :] if l == depth - 1 else None
        xf = _ffn(xf, ffn2_norm[l][None, :], f2i[l], f2o[l], fin)
    return xf.reshape(b, s, d)
```

```python
import functools

import jax
import jax.numpy as jnp
from jax import lax
from jax.experimental import pallas as pl
from jax.experimental.pallas import tpu as pltpu

F32 = jnp.float32
BF16 = jnp.bfloat16

NORM_EPS = 1e-6
RWKV_LN_EPS = 64e-5
GLA_TAU = 16.0
CHUNK = 64
HEAD = 64
LANES = 128
GROUP = 4
VMEM_LIMIT = 56 * 1024 * 1024

NN = ((1,), (0,))
NT = ((1,), (1,))
TN = ((0,), (0,))


def _dot(a, b, dims=NN):
    return lax.dot_general(a, b, (dims, ((), ())), preferred_element_type=F32)


def _split(x, n):
    if x.dtype == BF16:
        return [x]
    parts, rest = [], x
    for i in range(n):
        p = rest.astype(BF16)
        parts.append(p)
        if i + 1 < n:
            rest = rest - p.astype(F32)
    return parts


def _mm(a, b, dims=NN, pa=1, pb=1):
    aa, bb = _split(a, pa), _split(b, pb)
    order = max(len(aa), len(bb))
    out = None
    for i, x in enumerate(aa):
        for j, y in enumerate(bb):
            if i + j < order:
                t = _dot(x, y, dims)
                out = t if out is None else out + t
    return out


def _sigmoid(x):
    return 1.0 / (1.0 + jnp.exp(-x))


def _softplus(x):
    return jnp.maximum(x, 0.0) + jnp.log(1.0 + jnp.exp(-jnp.abs(x)))


def _rms(x, g):
    return x * lax.rsqrt(jnp.mean(x * x, axis=-1, keepdims=True) + NORM_EPS) * g


def _stack_heads(x, lane_lo):
    x = x.astype(BF16)
    zero = jnp.zeros_like(x)
    return jnp.concatenate([jnp.where(lane_lo, x, zero), jnp.where(lane_lo, zero, x)], axis=0)


def _segsum(x, ones):
    w = ones.shape[0]
    xb = x.astype(BF16)
    return jnp.concatenate([_dot(xb[:, i:i + w], ones) for i in range(0, x.shape[1], w)], axis=1)


def _const_spec(shape):
    nd = len(shape)
    return pl.BlockSpec(shape, lambda *_: (0,) * nd)


class _Layer:
    def __init__(self, arr, l):
        self.arr, self.l, self.shape = arr, l, arr.shape[1:]


def _wspec(p):
    if isinstance(p, _Layer):
        l, nd = p.l, len(p.shape)
        return pl.BlockSpec((None,) + tuple(p.shape), lambda *_: (l,) + (0,) * nd)
    return _const_spec(p.shape)


def _warg(p):
    return p.arr if isinstance(p, _Layer) else p


def _ffn_kernel(x_ref, g_ref, win_ref, wout_ref, *rest, d_ff, tf, final):
    if final:
        gf_ref, o_ref, acc_ref = rest
    else:
        o_ref, acc_ref = rest
    x = x_ref[...]
    h = _rms(x, g_ref[...]).astype(BF16)
    for c in range(d_ff // tf):
        gate = _dot(h, win_ref[:, c * tf:(c + 1) * tf])
        up = _dot(h, win_ref[:, d_ff + c * tf:d_ff + (c + 1) * tf])
        act = (gate * _sigmoid(gate) * up).astype(BF16)
        part = _dot(act, wout_ref[c * tf:(c + 1) * tf, :])
        if c == 0:
            acc_ref[...] = part
        else:
            acc_ref[...] += part
    y = x + 0.5 * acc_ref[...]
    if final:
        y = _rms(y, gf_ref[...])
    o_ref[...] = y


def _ffn(x, g, w_in, w_out, final_g=None, *, tm=512, tf=256):
    t, d = x.shape
    d_ff = w_out.shape[0]
    final = final_g is not None
    row = pl.BlockSpec((tm, d), lambda i: (i, 0))
    in_specs = [row, _const_spec((1, d)), _wspec(w_in), _wspec(w_out)]
    args = [x, g, _warg(w_in), _warg(w_out)]
    if final:
        in_specs.append(_const_spec((1, d)))
        args.append(final_g)
    return pl.pallas_call(
        functools.partial(_ffn_kernel, d_ff=d_ff, tf=tf, final=final),
        grid=(t // tm,),
        in_specs=in_specs,
        out_specs=row,
        out_shape=jax.ShapeDtypeStruct((t, d), F32),
        scratch_shapes=[pltpu.VMEM((tm, d), F32)],
        compiler_params=pltpu.CompilerParams(
            dimension_semantics=("arbitrary",), vmem_limit_bytes=VMEM_LIMIT),
        name="ffn_final" if final else "ffn",
    )(*args)


def _rwkv_phase1(probs, masks):
    lane_lo, strict, incl, levels, eye, blockdiag = masks
    n = range(len(probs))
    ar = [jnp.concatenate([q["at"], q["rt"]], axis=0).astype(BF16) for q in probs]
    ab = [_dot(ar[i], _stack_heads(probs[i]["bh"], lane_lo), NT) for i in n]
    yield
    ak = [_dot(ar[i], _stack_heads(probs[i]["kh"], lane_lo), NT) for i in n]
    a_ab = [jnp.where(strict, ab[i][:CHUNK], 0.0) for i in n]
    a_rb = [jnp.where(incl, ab[i][CHUNK:], 0.0).astype(BF16) for i in n]
    t = [eye + jnp.where(levels[0], a_ab[i], 0.0) for i in n]
    yield
    for m in levels[1:]:
        x = [_dot(t[i].astype(BF16), _stack_heads(jnp.where(m, a_ab[i], 0.0), lane_lo)) for i in n]
        yield
        t = [t[i] + _dot(x[i].astype(BF16), _stack_heads(t[i], lane_lo)) for i in n]
        yield
    a_ak = [jnp.where(strict, ak[i][:CHUNK], 0.0).astype(BF16) for i in n]
    a_rk = [jnp.where(incl, ak[i][CHUNK:], 0.0).astype(BF16) for i in n]
    vst = [_stack_heads(q["v"], lane_lo) for q in probs]
    av = [_dot(a_ak[i], vst[i]) for i in n]
    arkv = [_dot(a_rk[i], vst[i]) for i in n]
    yield
    vk = [jnp.where(blockdiag, _dot(probs[i]["v"].astype(BF16), probs[i]["kbar"].astype(BF16), TN), 0.0)
          for i in n]
    for i, q in enumerate(probs):
        q["store"](ar=ar[i], t=t[i].astype(BF16), arb=a_rb[i], bbar=q["bbar"].astype(BF16),
                   av=av[i], arkv=arkv[i], vk=vk[i])
    yield


def _interleave(*gens):
    live = list(gens)
    while live:
        for g in list(live):
            try:
                next(g)
            except StopIteration:
                live.remove(g)


def _pair_masks():
    row = lax.broadcasted_iota(jnp.int32, (CHUNK, LANES), 0)
    col = lax.broadcasted_iota(jnp.int32, (CHUNK, LANES), 1) % HEAD
    lane_lo = lax.broadcasted_iota(jnp.int32, (1, LANES), 1) < HEAD
    strict = row > col
    incl = row >= col
    levels = []
    for k in range(6):
        levels.append(((row >> (k + 1)) == (col >> (k + 1)))
                      & (((row >> k) & 1) == 1) & (((col >> k) & 1) == 0))
    eye = (row == col).astype(F32)
    r2 = lax.broadcasted_iota(jnp.int32, (LANES, LANES), 0) < HEAD
    c2 = lax.broadcasted_iota(jnp.int32, (LANES, LANES), 1) < HEAD
    blockdiag = r2 == c2
    return lane_lo, strict, incl, levels, eye, blockdiag


def _tri_ones():
    r = lax.broadcasted_iota(jnp.int32, (CHUNK, CHUNK), 0)
    c = lax.broadcasted_iota(jnp.int32, (CHUNK, CHUNK), 1)
    return (r >= c).astype(BF16)


def _rwkv_kernel(*refs, ts, width, has_vmix, tiles_per_seq, n_tiles):
    (x_ref, gmix_ref, wr_ref, mu_ref, wl_ref, bl_ref, g2_ref, kkw_ref, ka_ref, rk_ref,
     lnw_ref, lnb_ref, ones_ref) = refs[:13]
    refs = refs[13:]
    if has_vmix:
        v0_ref, v1_ref, v2_ref, vf_ref = refs[:4]
        refs = refs[4:]
        y_ref = refs[0]
        refs = refs[1:]
    else:
        y_ref, vf_out_ref = refs[:2]
        refs = refs[2:]
    (uprev, st, r_s, k_s, v_s, kk_s, b_s, lw_s, g_s, bg_s, y_s,
     ar_s, t_s, arb_s, bbar_s, av_s, arkv_s, vk_s, egc_s) = refs
    n_pairs = width // LANES
    step = pl.program_id(0)
    cur = step % 2
    prv = 1 - cur

    @pl.when(step == 0)
    def _():
        for ref in (g_s, bg_s, ar_s, t_s, arb_s, bbar_s, av_s, arkv_s, vk_s, egc_s):
            ref[1] = jnp.zeros(ref.shape[1:], ref.dtype)
        if not has_vmix:
            vf_out_ref[...] = jnp.zeros_like(vf_out_ref)

    @pl.when(step % tiles_per_seq == 0)
    def _():
        uprev[...] = jnp.zeros_like(uprev)

    @pl.when((step + tiles_per_seq - 1) % tiles_per_seq == 0)
    def _():
        st[...] = jnp.zeros_like(st)

    ones = ones_ref[...]

    def pre():
        x = x_ref[...]
        h = _rms(x, gmix_ref[...]).astype(BF16)
        u = _dot(h, wr_ref[...])
        yield
        rows = lax.broadcasted_iota(jnp.int32, (ts, 1), 0)
        u_prev = jnp.where(rows == 0, uprev[0:1, :], pltpu.roll(u, 1, 0))
        uprev[...] = pltpu.roll(u[ts - 8:ts, :], 1, 0)
        us = u + mu_ref[...] * (u_prev - u)
        r = us[:, 0:width]
        k = us[:, width:2 * width]
        v = us[:, 2 * width:3 * width]
        wa = us[:, 3 * width:3 * width + LANES]
        gl = us[:, 3 * width + LANES:3 * width + 2 * LANES]
        if has_vmix:
            z = _mm(_mm(v, v1_ref[...], NN, 2, 1), v2_ref[...], NN, 2, 1)
            v = v + (vf_ref[...] - v) * _sigmoid(v0_ref[...] + z)
        else:
            vf_out_ref[...] = jnp.where(step < n_tiles, v, vf_out_ref[...])
        yield
        lane = lax.broadcasted_iota(jnp.int32, (1, LANES), 1)
        act = jnp.where(lane < HEAD, jnp.tanh(wa), wa)
        lo = _mm(act, wl_ref[...], NN, 2, 1) + bl_ref[...]
        w_log = -_softplus(-lo[:, :width]) - 0.5
        lw_s[...] = -jnp.exp(w_log)
        a = _sigmoid(lo[:, width:])
        yield
        g = _mm(_sigmoid(gl), g2_ref[...], NN, 2, 1)
        kk = k * kkw_ref[...]
        ss = _segsum(kk * kk, ones)
        kk = kk / jnp.maximum(jnp.sqrt(ss), 1e-12)
        yield
        kh = k * (1.0 + (a - 1.0) * ka_ref[...])
        bonus = _segsum(r * kh * rk_ref[...], ones) * v
        r_s[...] = r
        k_s[...] = kh
        v_s[...] = v
        kk_s[...] = kk
        b_s[...] = kk * a
        g_s[cur] = g
        bg_s[cur] = bonus * g
        yield

    masks = _pair_masks()
    tri = _tri_ones()

    lane_lo, blockdiag = masks[0], masks[5]
    n_groups = ts // (CHUNK * GROUP)
    inv_n = 1.0 / HEAD

    def store_to(c, p):
        def store(ar, t, arb, bbar, av, arkv, vk):
            ar_s[cur, c, p] = ar
            t_s[cur, c, p] = t
            arb_s[cur, c, p] = arb
            bbar_s[cur, c, p] = bbar
            av_s[cur, c, p] = av
            arkv_s[cur, c, p] = arkv
            vk_s[cur, c, p] = vk
        return store

    def phase1(gi):
        probs = []
        for j in range(GROUP):
            c = gi * GROUP + j
            sl = slice(c * CHUNK, (c + 1) * CHUNK)
            lwc = lw_s[sl, :]
            gc = _mm(tri, lwc, NN, 1, 3)
            e_g = jnp.exp(gc)
            e_ng = jnp.exp(-gc)
            e_gm = jnp.exp(gc - lwc)
            e_gc = jnp.exp(gc[CHUNK - 1:CHUNK, :])
            egc_s[cur, c] = jnp.broadcast_to(e_gc, (8, width))
            rt = r_s[sl, :] * e_g
            at = -kk_s[sl, :] * e_gm
            bh = b_s[sl, :] * e_ng
            kh_ = k_s[sl, :] * e_ng
            bbar = bh * e_gc
            kbar = kh_ * e_gc
            vc = v_s[sl, :]
            for p in range(n_pairs):
                ls = slice(p * LANES, (p + 1) * LANES)
                probs.append(dict(rt=rt[:, ls], at=at[:, ls], bh=bh[:, ls], kh=kh_[:, ls],
                                  bbar=bbar[:, ls], kbar=kbar[:, ls], v=vc[:, ls], store=store_to(c, p)))
        yield from _rwkv_phase1(probs, masks)

    def phase2(gi):
        pairs = range(n_pairs)
        for c in range(gi * GROUP, (gi + 1) * GROUP):
            sl = slice(c * CHUNK, (c + 1) * CHUNK)
            s = [st[p] for p in pairs]
            wy = [_dot(ar_s[prv, c, p], s[p].astype(BF16), NT) for p in pairs]
            yield
            u = [_dot(t_s[prv, c, p], _stack_heads(wy[p][:CHUNK] + av_s[prv, c, p], lane_lo)) for p in pairs]
            yield
            y = [wy[p][CHUNK:] + _dot(arb_s[prv, c, p], _stack_heads(u[p], lane_lo)) + arkv_s[prv, c, p]
                 for p in pairs]
            e_gc = egc_s[prv, c][0:1, :]
            for p in pairs:
                ls = slice(p * LANES, (p + 1) * LANES)
                upd = _dot(u[p].astype(BF16), bbar_s[prv, c, p], TN)
                st[p] = s[p] * e_gc[:, ls] + jnp.where(blockdiag, upd, 0.0) + vk_s[prv, c, p]
                y_s[sl, ls] = y[p]
            yield

    def post(gi):
        sl = slice(gi * GROUP * CHUNK, (gi + 1) * GROUP * CHUNK)
        y = y_s[sl, :]
        mean = _segsum(y, ones) * inv_n
        yield
        d = y - mean
        var = _segsum(d * d, ones) * inv_n
        yn = d * lax.rsqrt(var + RWKV_LN_EPS) * lnw_ref[...] + lnb_ref[...]
        y_ref[sl, :] = yn * g_s[prv, sl, :] + bg_s[prv, sl, :]
        yield

    def stream_a():
        yield from pre()
        for gi in range(n_groups):
            yield from phase1(gi)

    def stream_b():
        for gi in range(n_groups):
            yield from phase2(gi)
            yield from post(gi)

    _interleave(stream_a(), stream_b())


def _lag_specs(n_tiles, ts):
    cur = lambda w: pl.BlockSpec((ts, w), lambda i: (jnp.minimum(i, n_tiles - 1), 0))
    lag = lambda w: pl.BlockSpec((ts, w), lambda i: (jnp.maximum(i - 1, 0), 0))
    return cur, lag


def _rwkv(xf, p, vmix, v_first, *, seq, ts=512):
    t, d = xf.shape
    width = p["kkw"].shape[-1]
    has_vmix = vmix is not None
    n_tiles = t // ts
    cur, lag = _lag_specs(n_tiles, ts)
    names = ("gmix", "wr", "mu", "wl", "bl", "g2", "kkw", "ka", "rk", "lnw", "lnb", "ones")
    args = [xf] + [_warg(p[n]) for n in names]
    in_specs = [cur(d)] + [_wspec(p[n]) for n in names]
    if has_vmix:
        args += [_warg(a) for a in vmix] + [v_first]
        in_specs += [_wspec(a) for a in vmix] + [cur(width)]
        out_shape = jax.ShapeDtypeStruct((t, width), F32)
        out_specs = lag(width)
    else:
        out_shape = (jax.ShapeDtypeStruct((t, width), F32),) * 2
        out_specs = (lag(width), cur(width))
    ucols = p["wr"].shape[1]
    tw = pltpu.VMEM((ts, width), F32)
    tw2 = pltpu.VMEM((2, ts, width), F32)
    nc, npairs = ts // CHUNK, width // LANES
    per = lambda rows, dt: pltpu.VMEM((2, nc, npairs, rows, LANES), dt)
    scratch = ([pltpu.VMEM((8, ucols), F32), pltpu.VMEM((npairs, LANES, LANES), F32)] + [tw] * 6 + [tw2, tw2, tw]
               + [per(2 * CHUNK, BF16), per(CHUNK, BF16), per(CHUNK, BF16), per(CHUNK, BF16),
                  per(CHUNK, F32), per(CHUNK, F32), per(LANES, F32), pltpu.VMEM((2, nc, 8, width), F32)])
    return pl.pallas_call(
        functools.partial(_rwkv_kernel, ts=ts, width=width, has_vmix=has_vmix, tiles_per_seq=seq // ts,
                          n_tiles=n_tiles),
        grid=(n_tiles + 1,),
        in_specs=in_specs,
        out_specs=out_specs,
        out_shape=out_shape,
        scratch_shapes=scratch,
        compiler_params=pltpu.CompilerParams(
            dimension_semantics=("arbitrary",), vmem_limit_bytes=VMEM_LIMIT),
        name="rwkv_vmix" if has_vmix else "rwkv",
    )(*args)


def _gla_kernel(x_ref, gmix_ref, wg_ref, conv_ref, aup_ref, ab_ref, nw_ref, y_ref,
                carry, st, q_s, k_s, v_s, la_s, o_s, go_s, qg_s, kv_s, el_s,
                *, ts, kw, vw, n_heads, tiles_per_seq):
    dv = vw // n_heads
    qkv_w = 2 * kw + vw
    step = pl.program_id(0)
    cur = step % 2
    prv = 1 - cur

    @pl.when(step == 0)
    def _():
        for ref in (o_s, go_s, qg_s, kv_s, el_s):
            ref[1] = jnp.zeros(ref.shape[1:], ref.dtype)

    @pl.when(step % tiles_per_seq == 0)
    def _():
        carry[...] = jnp.zeros_like(carry)

    @pl.when((step + tiles_per_seq - 1) % tiles_per_seq == 0)
    def _():
        st[...] = jnp.zeros_like(st)

    def pre_tile():
        x = x_ref[...]
        h = _rms(x, gmix_ref[...]).astype(BF16)
        u = _dot(h, wg_ref[...])
        yield
        qkv = u[:, :qkv_w]
        go = u[:, qkv_w:qkv_w + vw]
        al = u[:, qkv_w + vw:qkv_w + vw + LANES]
        go_s[cur] = go * _sigmoid(go)
        rows = lax.broadcasted_iota(jnp.int32, (ts, 1), 0)
        prev8 = carry[...]
        taps = conv_ref.shape[0]
        out = qkv * conv_ref[taps - 1:taps, :]
        for j in range(1, taps):
            shifted = pltpu.roll(qkv, j, 0)
            fill = pltpu.roll(prev8, j, 0)
            first = jnp.where(rows[:8] < j, fill, shifted[:8])
            shifted = jnp.concatenate([first, shifted[8:]], axis=0)
            out = out + shifted * conv_ref[taps - 1 - j:taps - j, :]
            yield
        carry[...] = qkv[ts - 8:ts, :]
        qkv = out * _sigmoid(out)
        log_a = _mm(al, aup_ref[...], NN, 2, 1) + ab_ref[...]
        log_a = (jnp.minimum(log_a, 0.0) - jnp.log(1.0 + jnp.exp(-jnp.abs(log_a)))) * (1.0 / GLA_TAU)
        q_s[...] = qkv[:, :kw] * (float(kw // n_heads) ** -0.5)
        k_s[...] = qkv[:, kw:2 * kw]
        v_s[...] = qkv[:, 2 * kw:].astype(BF16)
        la_s[...] = log_a
        yield

    lane_lo = lax.broadcasted_iota(jnp.int32, (1, LANES), 1) < HEAD
    r = lax.broadcasted_iota(jnp.int32, (CHUNK, CHUNK), 0)
    c = lax.broadcasted_iota(jnp.int32, (CHUNK, CHUNK), 1)
    lower = r >= c
    tri = lower.astype(BF16)
    n_groups = ts // (CHUNK * GROUP)
    n_kpairs = kw // LANES

    def phase1(gi):
        chunks = range(gi * GROUP, (gi + 1) * GROUP)
        pre = {}
        for ci in chunks:
            sl = slice(ci * CHUNK, (ci + 1) * CHUNK)
            gc = _mm(tri, la_s[sl, :], NN, 1, 3)
            e_g = jnp.exp(gc)
            e_ng = jnp.exp(-gc)
            e_last = jnp.exp(gc[CHUNK - 1:CHUNK, :])
            el_s[cur, ci] = jnp.broadcast_to(e_last, (8, kw))
            q = q_s[sl, :]
            k = k_s[sl, :]
            qg = q * e_g
            kg = k * e_ng
            qg_s[cur, ci] = qg.astype(BF16)
            pre[ci] = (qg, q * e_ng, kg.astype(BF16), (k * e_g).astype(BF16), (kg * e_last).astype(BF16))
        yield
        a_past = {(ci, pr): _dot(_stack_heads(pre[ci][0][:, pr * LANES:(pr + 1) * LANES], lane_lo),
                                 pre[ci][2][:, pr * LANES:(pr + 1) * LANES], NT)
                  for ci in chunks for pr in range(n_kpairs)}
        yield
        a_fut = {(ci, pr): _dot(_stack_heads(pre[ci][1][:, pr * LANES:(pr + 1) * LANES], lane_lo),
                                pre[ci][3][:, pr * LANES:(pr + 1) * LANES], NT)
                 for ci in chunks for pr in range(n_kpairs)}
        yield
        zero = jnp.zeros((CHUNK, LANES), BF16)
        for ci in chunks:
            sl = slice(ci * CHUNK, (ci + 1) * CHUNK)
            for hd in range(n_heads):
                pr, hh = divmod(hd, 2)
                hs = slice(hd * dv, (hd + 1) * dv)
                rs = slice(hh * CHUNK, (hh + 1) * CHUNK)
                a = jnp.where(lower, a_past[ci, pr][rs], a_fut[ci, pr][rs]).astype(BF16)
                vh = v_s[sl, hs]
                o_s[cur, sl, hs] = _dot(a, vh)
                kb = pre[ci][4][:, pr * LANES:(pr + 1) * LANES]
                kb = jnp.where(lane_lo, kb, zero) if hh == 0 else jnp.where(lane_lo, zero, kb)
                kv_s[cur, ci, hd] = _dot(vh, kb, TN)
            yield

    def phase2(gi):
        for ci in range(gi * GROUP, (gi + 1) * GROUP):
            sl = slice(ci * CHUNK, (ci + 1) * CHUNK)
            e_last = el_s[prv, ci][0:1, :]
            for hd in range(n_heads):
                pr = hd // 2
                ls = slice(pr * LANES, (pr + 1) * LANES)
                hs = slice(hd * dv, (hd + 1) * dv)
                s_h = st[hd]
                o_s[prv, sl, hs] += _dot(qg_s[prv, ci][:, ls], s_h.astype(BF16), NT)
                st[hd] = s_h * e_last[:, ls] + kv_s[prv, ci, hd]
            yield

    def post(gi):
        sl = slice(gi * GROUP * CHUNK, (gi + 1) * GROUP * CHUNK)
        for hd in range(n_heads):
            hs = slice(hd * dv, (hd + 1) * dv)
            o = o_s[prv, sl, hs]
            o = o * lax.rsqrt(jnp.mean(o * o, axis=-1, keepdims=True) + NORM_EPS) * nw_ref[:, hs]
            y_ref[sl, hs] = o * go_s[prv, sl, hs]
            yield

    def stream_a():
        yield from pre_tile()
        for gi in range(n_groups):
            yield from phase1(gi)

    def stream_b():
        for gi in range(n_groups):
            yield from phase2(gi)
            yield from post(gi)

    _interleave(stream_a(), stream_b())


def _gla(xf, p, *, seq, ts=512):
    t, d = xf.shape
    kw = p["aup"].shape[1]
    vw = p["nw"].shape[1]
    n_heads = kw // HEAD
    n_tiles = t // ts
    cur, lag = _lag_specs(n_tiles, ts)
    names = ("gmix", "wg", "conv", "aup", "ab", "nw")
    tkw = pltpu.VMEM((ts, kw), F32)
    tvw2 = pltpu.VMEM((2, ts, vw), F32)
    nc = ts // CHUNK
    return pl.pallas_call(
        functools.partial(_gla_kernel, ts=ts, kw=kw, vw=vw, n_heads=n_heads, tiles_per_seq=seq // ts),
        grid=(n_tiles + 1,),
        in_specs=[cur(d)] + [_wspec(p[n]) for n in names],
        out_specs=lag(vw),
        out_shape=jax.ShapeDtypeStruct((t, vw), F32),
        scratch_shapes=[pltpu.VMEM((8, 2 * kw + vw), F32), pltpu.VMEM((n_heads, vw // n_heads, LANES), F32),
                        tkw, tkw, pltpu.VMEM((ts, vw), BF16), tkw, tvw2, tvw2,
                        pltpu.VMEM((2, nc, CHUNK, kw), BF16),
                        pltpu.VMEM((2, nc, n_heads, vw // n_heads, LANES), F32),
                        pltpu.VMEM((2, nc, 8, kw), F32)],
        compiler_params=pltpu.CompilerParams(
            dimension_semantics=("arbitrary",), vmem_limit_bytes=VMEM_LIMIT),
        name="gla",
    )(xf, *[_warg(p[n]) for n in names])


def _xa_kv_kernel(mem_ref, g_ref, wkv_ref, k_ref, v_ref, *, width):
    m = _rms(mem_ref[0], g_ref[...]).astype(BF16)
    kv = _dot(m, wkv_ref[...])
    k_ref[0] = kv[:, :width].astype(BF16)
    v_ref[0] = kv[:, width:].astype(BF16)


def _xa_kv(mem, g, wkv):
    b, m, d = mem.shape
    width = wkv.shape[1] // 2
    blk = lambda w: pl.BlockSpec((1, m, w), lambda i: (i, 0, 0))
    return pl.pallas_call(
        functools.partial(_xa_kv_kernel, width=width),
        grid=(b,),
        in_specs=[blk(d), _const_spec(g.shape), _wspec(wkv)],
        out_specs=(blk(width), blk(width)),
        out_shape=(jax.ShapeDtypeStruct((b, m, width), BF16),) * 2,
        compiler_params=pltpu.CompilerParams(dimension_semantics=("arbitrary",), vmem_limit_bytes=VMEM_LIMIT),
        name="xa_kv",
    )(mem, g, _warg(wkv))


def _xa_kernel(x_ref, gmix_ref, wq_ref, k_ref, v_ref, y_ref, *, n_heads, hd):
    h = _rms(x_ref[0], gmix_ref[...]).astype(BF16)
    q = _dot(h, wq_ref[...]) * (float(hd) ** -0.5)
    for i in range(n_heads):
        hs = slice(i * hd, (i + 1) * hd)
        sc = _dot(q[:, hs].astype(BF16), k_ref[0, :, hs], NT)
        sc = sc - jnp.max(sc, axis=-1, keepdims=True)
        e = jnp.exp(sc)
        pr = e / jnp.sum(e, axis=-1, keepdims=True)
        y_ref[0, :, hs] = _dot(pr.astype(BF16), v_ref[0, :, hs])


def _xa(x3, gmix, wq, k, v, *, ts=512, hd=128):
    b, s, d = x3.shape
    m, width = k.shape[1], k.shape[2]
    tile = lambda w: pl.BlockSpec((1, ts, w), lambda i, j: (i, j, 0))
    kvs = pl.BlockSpec((1, m, width), lambda i, j: (i, 0, 0))
    return pl.pallas_call(
        functools.partial(_xa_kernel, n_heads=width // hd, hd=hd),
        grid=(b, s // ts),
        in_specs=[tile(d), _const_spec(gmix.shape), _wspec(wq), kvs, kvs],
        out_specs=tile(width),
        out_shape=jax.ShapeDtypeStruct((b, s, width), F32),
        compiler_params=pltpu.CompilerParams(
            dimension_semantics=("arbitrary", "arbitrary"), vmem_limit_bytes=VMEM_LIMIT),
        name="xa",
    )(x3, gmix, _warg(wq), k, v)


def _merge_kernel(x_ref, yr_ref, yg_ref, yx_ref, gmix_ref, wgate_ref, wb_ref, wo_ref, o_ref):
    x = x_ref[...]
    d = x.shape[-1]
    h = _rms(x, gmix_ref[...]).astype(BF16)
    merged = None
    for j, y_ref in enumerate((yr_ref, yg_ref, yx_ref)):
        gate = _sigmoid(_dot(h, wgate_ref[:, j * d:(j + 1) * d]))
        t = _dot(y_ref[...].astype(BF16), wb_ref[j]) * gate
        merged = t if merged is None else merged + t
    o_ref[...] = x + _dot(merged.astype(BF16), wo_ref[...])


def _merge(x, yr, yg, yx, gmix, wgate, wb, wo, *, tm=512):
    t, d = x.shape
    bw = yr.shape[1]
    row = lambda w: pl.BlockSpec((tm, w), lambda i: (i, 0))
    return pl.pallas_call(
        _merge_kernel,
        grid=(t // tm,),
        in_specs=[row(d), row(bw), row(bw), row(bw), _const_spec(gmix.shape), _wspec(wgate),
                  _wspec(wb), _wspec(wo)],
        out_specs=row(d),
        out_shape=jax.ShapeDtypeStruct((t, d), F32),
        compiler_params=pltpu.CompilerParams(dimension_semantics=("arbitrary",), vmem_limit_bytes=VMEM_LIMIT),
        name="merge",
    )(x, yr, yg, yx, gmix, _warg(wgate), _warg(wb), _warg(wo))


def _block_ones(width, block):
    i = jnp.arange(width) // block
    return (i[:, None] == i[None, :]).astype(BF16)


def kernel(x, mem, ffn1_norm, ffn1_w_in, ffn1_w_out, mix_norm, mem_norm, w_in, rwkv_mu, rwkv_w0, rwkv_w2, rwkv_a0, rwkv_a2, rwkv_g2, rwkv_k_k, rwkv_k_a, rwkv_r_k, rwkv_ln_w, rwkv_ln_b, rwkv_v0, rwkv_v1, rwkv_v2, gla_conv, gla_a_up, gla_a_bias, gla_norm, xa_w_kv, w_branch, w_out, ffn2_norm, ffn2_w_in, ffn2_w_out, final_norm):
    b, s, d = x.shape
    depth = w_in.shape[0]
    rw = rwkv_w0.shape[1]
    dl, al_ = rwkv_w2.shape[1], rwkv_a2.shape[1]
    gl_ = rwkv_g2.shape[1]
    kw = gla_a_up.shape[2]
    vw = gla_norm.shape[1]
    glora = gla_a_up.shape[1]
    xw = xa_w_kv.shape[2] // 2
    rcols = 3 * rw + dl + al_ + gl_
    gcols = 2 * kw + vw + glora + vw
    assert dl + al_ == LANES and gl_ == LANES and dl == HEAD

    row1 = lambda a: a[:, None, :]
    w_r = w_in[:, :, :rcols].astype(BF16)
    g0 = rcols
    w_gqkv = w_in[:, :, g0:g0 + 2 * kw + vw]
    w_gal = w_in[:, :, g0 + 2 * kw + vw:g0 + 2 * kw + vw + glora]
    w_ggo = w_in[:, :, g0 + 2 * kw + vw + glora:g0 + gcols]
    w_g = jnp.concatenate([w_gqkv, w_ggo, jnp.pad(w_gal, ((0, 0), (0, 0), (0, LANES - glora)))], axis=-1).astype(BF16)
    w_xq = w_in[:, :, g0 + gcols:g0 + gcols + xw].astype(BF16)
    w_gate = w_in[:, :, g0 + gcols + xw:].astype(BF16)
    zl = jnp.zeros((depth, dl, rw), F32)
    w_lora = jnp.concatenate([jnp.concatenate([rwkv_w2, zl], axis=-1),
                              jnp.concatenate([zl, rwkv_a2], axis=-1)], axis=1).astype(BF16)
    b_lora = row1(jnp.concatenate([rwkv_w0, rwkv_a0], axis=-1))
    mv = rwkv_v1.shape[2]
    v1p = jnp.pad(rwkv_v1, ((0, 0), (0, 0), (0, LANES - mv))).astype(BF16)
    v2p = jnp.pad(rwkv_v2, ((0, 0), (0, LANES - mv), (0, 0))).astype(BF16)
    aup = jnp.pad(gla_a_up, ((0, 0), (0, LANES - glora), (0, 0))).astype(BF16)
    ones = _block_ones(2 * LANES, HEAD)
    f1i, f1o = ffn1_w_in.astype(BF16), ffn1_w_out.astype(BF16)
    f2i, f2o = ffn2_w_in.astype(BF16), ffn2_w_out.astype(BF16)
    wkv = xa_w_kv.astype(BF16)
    wb = w_branch.astype(BF16)
    wo = w_out.astype(BF16)
    g2 = rwkv_g2.astype(BF16)

    xf = x.reshape(b * s, d)
    v_first = None
    for l in range(depth):
        gmix = mix_norm[l][None, :]
        xf = _ffn(xf, ffn1_norm[l][None, :], _Layer(f1i, l), _Layer(f1o, l))
        x3 = xf.reshape(b, s, d)
        rp = dict(gmix=gmix, wr=_Layer(w_r, l), mu=rwkv_mu[l][None, :], wl=_Layer(w_lora, l), bl=b_lora[l],
                  g2=_Layer(g2, l), kkw=rwkv_k_k[l][None, :], ka=rwkv_k_a[l][None, :],
                  rk=rwkv_r_k[l].reshape(1, rw), lnw=rwkv_ln_w[l][None, :], lnb=rwkv_ln_b[l][None, :], ones=ones)
        if l == 0:
            y_r, v_first = _rwkv(xf, rp, None, None, seq=s)
        else:
            vmix = (rwkv_v0[l - 1][None, :], _Layer(v1p, l - 1), _Layer(v2p, l - 1))
            y_r = _rwkv(xf, rp, vmix, v_first, seq=s)
        gp = dict(gmix=gmix, wg=_Layer(w_g, l), conv=gla_conv[l], aup=_Layer(aup, l),
                  ab=gla_a_bias[l][None, :], nw=gla_norm[l][None, :])
        y_g = _gla(xf, gp, seq=s)
        k_m, v_m = _xa_kv(mem, mem_norm[l][None, :], _Layer(wkv, l))
        y_x = _xa(x3, gmix, _Layer(w_xq, l), k_m, v_m)
        xf = _merge(xf, y_r, y_g, y_x.reshape(b * s, xw),
                    gmix, _Layer(w_gate, l), _Layer(wb, l), _Layer(wo, l))
        fin = final_norm[None, :] if l == depth - 1 else None
        xf = _ffn(xf, ffn2_norm[l][None, :], _Layer(f2i, l), _Layer(f2o, l), fin)
    return xf.reshape(b, s, d)
```

```python
import functools

import jax
import jax.numpy as jnp
from jax import lax
from jax.experimental import pallas as pl
from jax.experimental.pallas import tpu as pltpu

F32 = jnp.float32
BF16 = jnp.bfloat16

NORM_EPS = 1e-6
RWKV_LN_EPS = 64e-5
GLA_TAU = 16.0
CHUNK = 64
HEAD = 64
LANES = 128
PACK = 128
GROUP = 4
VMEM_LIMIT = 56 * 1024 * 1024

NN = ((1,), (0,))
NT = ((1,), (1,))
TN = ((0,), (0,))


def _dot(a, b, dims=NN):
    return lax.dot_general(a, b, (dims, ((), ())), preferred_element_type=F32)


def _split(x, n):
    if x.dtype == BF16:
        return [x]
    parts, rest = [], x
    for i in range(n):
        p = rest.astype(BF16)
        parts.append(p)
        if i + 1 < n:
            rest = rest - p.astype(F32)
    return parts


def _mm(a, b, dims=NN, pa=1, pb=1):
    aa, bb = _split(a, pa), _split(b, pb)
    order = max(len(aa), len(bb))
    out = None
    for i, x in enumerate(aa):
        for j, y in enumerate(bb):
            if i + j < order:
                t = _dot(x, y, dims)
                out = t if out is None else out + t
    return out


def _sigmoid(x):
    return 1.0 / (1.0 + jnp.exp(-x))


def _softplus(x):
    return jnp.maximum(x, 0.0) + jnp.log(1.0 + jnp.exp(-jnp.abs(x)))


def _rms(x, g):
    return x * lax.rsqrt(jnp.mean(x * x, axis=-1, keepdims=True) + NORM_EPS) * g


def _stack_heads(x, head_lanes):
    x = x.astype(BF16)
    zero = jnp.zeros_like(x)
    return jnp.concatenate([jnp.where(m, x, zero) for m in head_lanes], axis=0)


def _head_lanes(width):
    lane = lax.broadcasted_iota(jnp.int32, (1, width), 1) // HEAD
    return [lane == h for h in range(width // HEAD)]


def _segsum(x, ones):
    w = ones.shape[0]
    xb = x.astype(BF16)
    return jnp.concatenate([_dot(xb[:, i:i + w], ones) for i in range(0, x.shape[1], w)], axis=1)


def _const_spec(shape):
    nd = len(shape)
    return pl.BlockSpec(shape, lambda *_: (0,) * nd)


class _Layer:
    def __init__(self, arr, l):
        self.arr, self.l, self.shape = arr, l, arr.shape[1:]


def _wspec(p):
    if isinstance(p, _Layer):
        l, nd = p.l, len(p.shape)
        return pl.BlockSpec((None,) + tuple(p.shape), lambda *_: (l,) + (0,) * nd)
    return _const_spec(p.shape)


def _warg(p):
    return p.arr if isinstance(p, _Layer) else p


def _ffn_kernel(x_ref, g_ref, win_ref, wout_ref, *rest, d_ff, tf, post):
    if post == "final":
        gf_ref, o_ref, acc_ref = rest
    elif post == "emit":
        gf_ref, o_ref, h_ref, acc_ref = rest
    else:
        o_ref, acc_ref = rest
    x = x_ref[...]
    h = _rms(x, g_ref[...]).astype(BF16)
    for c in range(d_ff // tf):
        gate = _dot(h, win_ref[:, c * tf:(c + 1) * tf])
        up = _dot(h, win_ref[:, d_ff + c * tf:d_ff + (c + 1) * tf])
        act = (gate * _sigmoid(gate) * up).astype(BF16)
        part = _dot(act, wout_ref[c * tf:(c + 1) * tf, :])
        if c == 0:
            acc_ref[...] = part
        else:
            acc_ref[...] += part
    y = x + 0.5 * acc_ref[...]
    if post == "final":
        y = _rms(y, gf_ref[...])
    elif post == "emit":
        h_ref[...] = _rms(y, gf_ref[...]).astype(BF16)
    o_ref[...] = y


def _ffn(x, g, w_in, w_out, post_g=None, post="none", *, tm=512, tf=256):
    t, d = x.shape
    d_ff = w_out.shape[0]
    row = pl.BlockSpec((tm, d), lambda i: (i, 0))
    in_specs = [row, _const_spec((1, d)), _wspec(w_in), _wspec(w_out)]
    args = [x, g, _warg(w_in), _warg(w_out)]
    out_specs, out_shape = row, jax.ShapeDtypeStruct((t, d), F32)
    if post != "none":
        in_specs.append(_const_spec((1, d)))
        args.append(post_g)
    if post == "emit":
        out_specs, out_shape = (row, row), (out_shape, jax.ShapeDtypeStruct((t, d), BF16))
    return pl.pallas_call(
        functools.partial(_ffn_kernel, d_ff=d_ff, tf=tf, post=post),
        grid=(t // tm,),
        in_specs=in_specs,
        out_specs=out_specs,
        out_shape=out_shape,
        scratch_shapes=[pltpu.VMEM((tm, d), F32)],
        compiler_params=pltpu.CompilerParams(
            dimension_semantics=("arbitrary",), vmem_limit_bytes=VMEM_LIMIT),
        name="ffn_" + post,
    )(*args)


def _rwkv_phase1(probs, masks):
    lane_lo, strict, incl, levels, eye, blockdiag = masks
    n = range(len(probs))
    ar = [jnp.concatenate([q["at"], q["rt"]], axis=0).astype(BF16) for q in probs]
    ab = [_dot(ar[i], _stack_heads(probs[i]["bh"], lane_lo), NT) for i in n]
    yield
    ak = [_dot(ar[i], _stack_heads(probs[i]["kh"], lane_lo), NT) for i in n]
    a_ab = [jnp.where(strict, ab[i][:CHUNK], 0.0) for i in n]
    a_rb = [jnp.where(incl, ab[i][CHUNK:], 0.0).astype(BF16) for i in n]
    t = [eye + jnp.where(levels[0], a_ab[i], 0.0) for i in n]
    yield
    for m in levels[1:]:
        x = [_dot(t[i].astype(BF16), _stack_heads(jnp.where(m, a_ab[i], 0.0), lane_lo)) for i in n]
        yield
        t = [t[i] + _dot(x[i].astype(BF16), _stack_heads(t[i], lane_lo)) for i in n]
        yield
    a_ak = [jnp.where(strict, ak[i][:CHUNK], 0.0).astype(BF16) for i in n]
    a_rk = [jnp.where(incl, ak[i][CHUNK:], 0.0).astype(BF16) for i in n]
    vst = [_stack_heads(q["v"], lane_lo) for q in probs]
    av = [_dot(a_ak[i], vst[i]) for i in n]
    arkv = [_dot(a_rk[i], vst[i]) for i in n]
    yield
    vk = [jnp.where(blockdiag, _dot(probs[i]["v"].astype(BF16), probs[i]["kbar"].astype(BF16), TN), 0.0)
          for i in n]
    for i, q in enumerate(probs):
        q["store"](ar=ar[i], t=t[i].astype(BF16), arb=a_rb[i], bbar=q["bbar"].astype(BF16),
                   av=av[i], arkv=arkv[i], vk=vk[i])
    yield


def _interleave(*gens):
    live = list(gens)
    while live:
        for g in list(live):
            try:
                next(g)
            except StopIteration:
                live.remove(g)


def _pack_masks():
    row = lax.broadcasted_iota(jnp.int32, (CHUNK, PACK), 0)
    col = lax.broadcasted_iota(jnp.int32, (CHUNK, PACK), 1) % HEAD
    strict = row > col
    incl = row >= col
    levels = []
    k = 0
    while (1 << k) < CHUNK:
        levels.append(((row >> (k + 1)) == (col >> (k + 1)))
                      & (((row >> k) & 1) == 1) & (((col >> k) & 1) == 0))
        k += 1
    eye = (row == col).astype(F32)
    r2 = lax.broadcasted_iota(jnp.int32, (PACK, PACK), 0) // HEAD
    c2 = lax.broadcasted_iota(jnp.int32, (PACK, PACK), 1) // HEAD
    blockdiag = r2 == c2
    return _head_lanes(PACK), strict, incl, levels, eye, blockdiag


def _tri_ones():
    r = lax.broadcasted_iota(jnp.int32, (CHUNK, CHUNK), 0)
    c = lax.broadcasted_iota(jnp.int32, (CHUNK, CHUNK), 1)
    return (r >= c).astype(BF16)


def _rwkv_kernel(*refs, ts, width, has_vmix, tiles_per_seq, n_tiles):
    (h_ref, wr_ref, mu_ref, wl_ref, bl_ref, g2_ref, kkw_ref, ka_ref, rk_ref,
     lnw_ref, lnb_ref, ones_ref) = refs[:12]
    refs = refs[12:]
    if has_vmix:
        v0_ref, v1_ref, v2_ref, vf_ref = refs[:4]
        refs = refs[4:]
        y_ref = refs[0]
        refs = refs[1:]
    else:
        y_ref, vf_out_ref = refs[:2]
        refs = refs[2:]
    (uprev, st, r_s, k_s, v_s, kk_s, b_s, lw_s, g_s, bg_s, y_s,
     ar_s, t_s, arb_s, bbar_s, av_s, arkv_s, vk_s, egc_s) = refs
    n_pairs = width // PACK
    step = pl.program_id(0)
    cur = step % 2
    prv = 1 - cur

    @pl.when(step == 0)
    def _():
        for ref in (g_s, bg_s, ar_s, t_s, arb_s, bbar_s, av_s, arkv_s, vk_s, egc_s):
            ref[1] = jnp.zeros(ref.shape[1:], ref.dtype)
        if not has_vmix:
            vf_out_ref[...] = jnp.zeros_like(vf_out_ref)

    @pl.when(step % tiles_per_seq == 0)
    def _():
        uprev[...] = jnp.zeros_like(uprev)

    @pl.when((step + tiles_per_seq - 1) % tiles_per_seq == 0)
    def _():
        st[...] = jnp.zeros_like(st)

    ones = ones_ref[...]

    def pre():
        u = _dot(h_ref[...], wr_ref[...])
        yield
        rolled = pltpu.roll(u, 1, 0)
        first = jnp.where(lax.broadcasted_iota(jnp.int32, (8, 1), 0) == 0, uprev[0:1, :], rolled[:8])
        u_prev = jnp.concatenate([first, rolled[8:]], axis=0)
        uprev[...] = pltpu.roll(u[ts - 8:ts, :], 1, 0)
        us = u + mu_ref[...] * (u_prev - u)
        r = us[:, 0:width]
        k = us[:, width:2 * width]
        v = us[:, 2 * width:3 * width]
        wa = us[:, 3 * width:3 * width + LANES]
        gl = us[:, 3 * width + LANES:3 * width + 2 * LANES]
        if has_vmix:
            z = _mm(_mm(v, v1_ref[...]), v2_ref[...])
            v = v + (vf_ref[...] - v) * _sigmoid(v0_ref[...] + z)
        else:
            vf_out_ref[...] = jnp.where(step < n_tiles, v, vf_out_ref[...])
        yield
        lane = lax.broadcasted_iota(jnp.int32, (1, LANES), 1)
        act = jnp.where(lane < HEAD, jnp.tanh(wa), wa)
        lo = _mm(act, wl_ref[...]) + bl_ref[...]
        w_log = -_softplus(-lo[:, :width]) - 0.5
        lw_s[...] = -jnp.exp(w_log)
        a = _sigmoid(lo[:, width:])
        yield
        g = _mm(_sigmoid(gl), g2_ref[...])
        kk = k * kkw_ref[...]
        ss = _segsum(kk * kk, ones)
        kk = kk * jnp.minimum(lax.rsqrt(ss), 1e12)
        yield
        kh = k * (1.0 + (a - 1.0) * ka_ref[...])
        bonus = _segsum(r * kh * rk_ref[...], ones) * v
        r_s[...] = r
        k_s[...] = kh
        v_s[...] = v
        kk_s[...] = kk
        b_s[...] = kk * a
        g_s[cur] = g
        bg_s[cur] = bonus * g
        yield

    masks = _pack_masks()
    tri = _tri_ones()

    lane_lo, blockdiag = masks[0], masks[5]
    n_groups = ts // (CHUNK * GROUP)
    inv_n = 1.0 / HEAD

    def store_to(c, p):
        def store(ar, t, arb, bbar, av, arkv, vk):
            ar_s[cur, c, p] = ar
            t_s[cur, c, p] = t
            arb_s[cur, c, p] = arb
            bbar_s[cur, c, p] = bbar
            av_s[cur, c, p] = av
            arkv_s[cur, c, p] = arkv
            vk_s[cur, c, p] = vk
        return store

    def phase1(gi):
        probs = []
        for j in range(GROUP):
            c = gi * GROUP + j
            sl = slice(c * CHUNK, (c + 1) * CHUNK)
            lwc = lw_s[sl, :]
            gc = _mm(tri, lwc, NN, 1, 3)
            e_g = jnp.exp(gc)
            e_ng = jnp.exp(-gc)
            e_gm = jnp.exp(gc - lwc)
            e_gc = jnp.exp(gc[CHUNK - 1:CHUNK, :])
            egc_s[cur, c] = jnp.broadcast_to(e_gc, (8, width))
            rt = r_s[sl, :] * e_g
            at = -kk_s[sl, :] * e_gm
            bh = b_s[sl, :] * e_ng
            kh_ = k_s[sl, :] * e_ng
            bbar = bh * e_gc
            kbar = kh_ * e_gc
            vc = v_s[sl, :]
            for p in range(n_pairs):
                ls = slice(p * PACK, (p + 1) * PACK)
                probs.append(dict(rt=rt[:, ls], at=at[:, ls], bh=bh[:, ls], kh=kh_[:, ls],
                                  bbar=bbar[:, ls], kbar=kbar[:, ls], v=vc[:, ls], store=store_to(c, p)))
        yield from _rwkv_phase1(probs, masks)

    def phase2(gi):
        pairs = range(n_pairs)
        for c in range(gi * GROUP, (gi + 1) * GROUP):
            sl = slice(c * CHUNK, (c + 1) * CHUNK)
            s = [st[p] for p in pairs]
            wy = [_dot(ar_s[prv, c, p], s[p].astype(BF16), NT) for p in pairs]
            yield
            u = [_dot(t_s[prv, c, p], _stack_heads(wy[p][:CHUNK] + av_s[prv, c, p], lane_lo)) for p in pairs]
            yield
            y = [wy[p][CHUNK:] + _dot(arb_s[prv, c, p], _stack_heads(u[p], lane_lo)) + arkv_s[prv, c, p]
                 for p in pairs]
            e_gc = egc_s[prv, c][0:1, :]
            for p in pairs:
                ls = slice(p * PACK, (p + 1) * PACK)
                upd = _dot(u[p].astype(BF16), bbar_s[prv, c, p], TN)
                st[p] = s[p] * e_gc[:, ls] + jnp.where(blockdiag, upd, 0.0) + vk_s[prv, c, p]
                y_s[sl, ls] = y[p]
            yield

    def post(gi):
        sl = slice(gi * GROUP * CHUNK, (gi + 1) * GROUP * CHUNK)
        y = y_s[sl, :]
        mean = _segsum(y, ones) * inv_n
        yield
        d = y - mean
        var = _segsum(d * d, ones) * inv_n
        yn = d * lax.rsqrt(var + RWKV_LN_EPS) * lnw_ref[...] + lnb_ref[...]
        y_ref[sl, :] = (yn * g_s[prv, sl, :] + bg_s[prv, sl, :]).astype(BF16)
        yield

    def stream_a():
        yield from pre()
        for gi in range(n_groups):
            yield from phase1(gi)

    def stream_b():
        for gi in range(n_groups):
            yield from phase2(gi)
            yield from post(gi)

    _interleave(stream_a(), stream_b())


def _lag_specs(n_tiles, ts):
    cur = lambda w: pl.BlockSpec((ts, w), lambda i: (jnp.minimum(i, n_tiles - 1), 0))
    lag = lambda w: pl.BlockSpec((ts, w), lambda i: (jnp.maximum(i - 1, 0), 0))
    return cur, lag


def _rwkv(xf, p, vmix, v_first, *, seq, ts=512):
    t, d = xf.shape
    width = p["kkw"].shape[-1]
    has_vmix = vmix is not None
    n_tiles = t // ts
    cur, lag = _lag_specs(n_tiles, ts)
    names = ("wr", "mu", "wl", "bl", "g2", "kkw", "ka", "rk", "lnw", "lnb", "ones")
    args = [xf] + [_warg(p[n]) for n in names]
    in_specs = [cur(d)] + [_wspec(p[n]) for n in names]
    if has_vmix:
        args += [_warg(a) for a in vmix] + [v_first]
        in_specs += [_wspec(a) for a in vmix] + [cur(width)]
        out_shape = jax.ShapeDtypeStruct((t, width), BF16)
        out_specs = lag(width)
    else:
        out_shape = (jax.ShapeDtypeStruct((t, width), BF16), jax.ShapeDtypeStruct((t, width), F32))
        out_specs = (lag(width), cur(width))
    ucols = p["wr"].shape[1]
    tw = pltpu.VMEM((ts, width), F32)
    tw2 = pltpu.VMEM((2, ts, width), F32)
    nc, npairs = ts // CHUNK, width // PACK
    per = lambda rows, dt: pltpu.VMEM((2, nc, npairs, rows, PACK), dt)
    scratch = ([pltpu.VMEM((8, ucols), F32), pltpu.VMEM((npairs, PACK, PACK), F32)] + [tw] * 6 + [tw2, tw2, tw]
               + [per(2 * CHUNK, BF16), per(CHUNK, BF16), per(CHUNK, BF16), per(CHUNK, BF16),
                  per(CHUNK, F32), per(CHUNK, F32), per(PACK, F32), pltpu.VMEM((2, nc, 8, width), F32)])
    return pl.pallas_call(
        functools.partial(_rwkv_kernel, ts=ts, width=width, has_vmix=has_vmix, tiles_per_seq=seq // ts,
                          n_tiles=n_tiles),
        grid=(n_tiles + 1,),
        in_specs=in_specs,
        out_specs=out_specs,
        out_shape=out_shape,
        scratch_shapes=scratch,
        compiler_params=pltpu.CompilerParams(
            dimension_semantics=("arbitrary",), vmem_limit_bytes=VMEM_LIMIT),
        name="rwkv_vmix" if has_vmix else "rwkv",
    )(*args)


def _gla_kernel(h_ref, wg_ref, conv_ref, aup_ref, ab_ref, nw_ref, y_ref,
                carry, st, q_s, k_s, v_s, la_s, o_s, go_s, qg_s, kv_s, el_s,
                *, ts, kw, vw, n_heads, tiles_per_seq):
    dv = vw // n_heads
    qkv_w = 2 * kw + vw
    step = pl.program_id(0)
    cur = step % 2
    prv = 1 - cur

    @pl.when(step == 0)
    def _():
        for ref in (o_s, go_s, qg_s, kv_s, el_s):
            ref[1] = jnp.zeros(ref.shape[1:], ref.dtype)

    @pl.when(step % tiles_per_seq == 0)
    def _():
        carry[...] = jnp.zeros_like(carry)

    @pl.when((step + tiles_per_seq - 1) % tiles_per_seq == 0)
    def _():
        st[...] = jnp.zeros_like(st)

    def pre_tile():
        u = _dot(h_ref[...], wg_ref[...])
        yield
        qkv = u[:, :qkv_w]
        go = u[:, qkv_w:qkv_w + vw]
        al = u[:, qkv_w + vw:qkv_w + vw + LANES]
        go_s[cur] = go * _sigmoid(go)
        rows = lax.broadcasted_iota(jnp.int32, (ts, 1), 0)
        prev8 = carry[...]
        taps = conv_ref.shape[0]
        out = qkv * conv_ref[taps - 1:taps, :]
        for j in range(1, taps):
            shifted = pltpu.roll(qkv, j, 0)
            fill = pltpu.roll(prev8, j, 0)
            first = jnp.where(rows[:8] < j, fill, shifted[:8])
            shifted = jnp.concatenate([first, shifted[8:]], axis=0)
            out = out + shifted * conv_ref[taps - 1 - j:taps - j, :]
            yield
        carry[...] = qkv[ts - 8:ts, :]
        qkv = out * _sigmoid(out)
        log_a = _mm(al, aup_ref[...]) + ab_ref[...]
        log_a = (jnp.minimum(log_a, 0.0) - jnp.log(1.0 + jnp.exp(-jnp.abs(log_a)))) * (1.0 / GLA_TAU)
        q_s[...] = qkv[:, :kw] * (float(kw // n_heads) ** -0.5)
        k_s[...] = qkv[:, kw:2 * kw]
        v_s[...] = qkv[:, 2 * kw:].astype(BF16)
        la_s[...] = log_a
        yield

    pair_lanes = _head_lanes(LANES)
    lane_lo = pair_lanes[0]
    r = lax.broadcasted_iota(jnp.int32, (CHUNK, CHUNK), 0)
    c = lax.broadcasted_iota(jnp.int32, (CHUNK, CHUNK), 1)
    lower = r >= c
    tri = lower.astype(BF16)
    n_groups = ts // (CHUNK * GROUP)
    n_kpairs = kw // LANES

    def phase1(gi):
        chunks = range(gi * GROUP, (gi + 1) * GROUP)
        pre = {}
        for ci in chunks:
            sl = slice(ci * CHUNK, (ci + 1) * CHUNK)
            gc = _mm(tri, la_s[sl, :], NN, 1, 3)
            e_g = jnp.exp(gc)
            e_ng = jnp.exp(-gc)
            e_last = jnp.exp(gc[CHUNK - 1:CHUNK, :])
            el_s[cur, ci] = jnp.broadcast_to(e_last, (8, kw))
            q = q_s[sl, :]
            k = k_s[sl, :]
            qg = q * e_g
            kg = k * e_ng
            qg_s[cur, ci] = qg.astype(BF16)
            pre[ci] = (qg, q * e_ng, kg.astype(BF16), (k * e_g).astype(BF16), (kg * e_last).astype(BF16))
        yield
        a_past = {(ci, pr): _dot(_stack_heads(pre[ci][0][:, pr * LANES:(pr + 1) * LANES], pair_lanes),
                                 pre[ci][2][:, pr * LANES:(pr + 1) * LANES], NT)
                  for ci in chunks for pr in range(n_kpairs)}
        yield
        a_fut = {(ci, pr): _dot(_stack_heads(pre[ci][1][:, pr * LANES:(pr + 1) * LANES], pair_lanes),
                                pre[ci][3][:, pr * LANES:(pr + 1) * LANES], NT)
                 for ci in chunks for pr in range(n_kpairs)}
        yield
        zero = jnp.zeros((CHUNK, LANES), BF16)
        for ci in chunks:
            sl = slice(ci * CHUNK, (ci + 1) * CHUNK)
            for hd in range(n_heads):
                pr, hh = divmod(hd, 2)
                hs = slice(hd * dv, (hd + 1) * dv)
                rs = slice(hh * CHUNK, (hh + 1) * CHUNK)
                a = jnp.where(lower, a_past[ci, pr][rs], a_fut[ci, pr][rs]).astype(BF16)
                vh = v_s[sl, hs]
                o_s[cur, sl, hs] = _dot(a, vh)
                kb = pre[ci][4][:, pr * LANES:(pr + 1) * LANES]
                kb = jnp.where(lane_lo, kb, zero) if hh == 0 else jnp.where(lane_lo, zero, kb)
                kv_s[cur, ci, hd] = _dot(vh, kb, TN)
            yield

    def phase2(gi):
        for ci in range(gi * GROUP, (gi + 1) * GROUP):
            sl = slice(ci * CHUNK, (ci + 1) * CHUNK)
            e_last = el_s[prv, ci][0:1, :]
            for hd in range(n_heads):
                pr = hd // 2
                ls = slice(pr * LANES, (pr + 1) * LANES)
                hs = slice(hd * dv, (hd + 1) * dv)
                s_h = st[hd]
                o_s[prv, sl, hs] += _dot(qg_s[prv, ci][:, ls], s_h.astype(BF16), NT)
                st[hd] = s_h * e_last[:, ls] + kv_s[prv, ci, hd]
            yield

    def post(gi):
        sl = slice(gi * GROUP * CHUNK, (gi + 1) * GROUP * CHUNK)
        for hd in range(n_heads):
            hs = slice(hd * dv, (hd + 1) * dv)
            o = o_s[prv, sl, hs]
            o = o * lax.rsqrt(jnp.mean(o * o, axis=-1, keepdims=True) + NORM_EPS) * nw_ref[:, hs]
            y_ref[sl, hs] = (o * go_s[prv, sl, hs]).astype(BF16)
            yield

    def stream_a():
        yield from pre_tile()
        for gi in range(n_groups):
            yield from phase1(gi)

    def stream_b():
        for gi in range(n_groups):
            yield from phase2(gi)
            yield from post(gi)

    _interleave(stream_a(), stream_b())


def _gla(xf, p, *, seq, ts=512):
    t, d = xf.shape
    kw = p["aup"].shape[1]
    vw = p["nw"].shape[1]
    n_heads = kw // HEAD
    n_tiles = t // ts
    cur, lag = _lag_specs(n_tiles, ts)
    names = ("wg", "conv", "aup", "ab", "nw")
    tkw = pltpu.VMEM((ts, kw), F32)
    tvw2 = pltpu.VMEM((2, ts, vw), F32)
    nc = ts // CHUNK
    return pl.pallas_call(
        functools.partial(_gla_kernel, ts=ts, kw=kw, vw=vw, n_heads=n_heads, tiles_per_seq=seq // ts),
        grid=(n_tiles + 1,),
        in_specs=[cur(d)] + [_wspec(p[n]) for n in names],
        out_specs=lag(vw),
        out_shape=jax.ShapeDtypeStruct((t, vw), BF16),
        scratch_shapes=[pltpu.VMEM((8, 2 * kw + vw), F32), pltpu.VMEM((n_heads, vw // n_heads, LANES), F32),
                        tkw, tkw, pltpu.VMEM((ts, vw), BF16), tkw, tvw2, tvw2,
                        pltpu.VMEM((2, nc, CHUNK, kw), BF16),
                        pltpu.VMEM((2, nc, n_heads, vw // n_heads, LANES), F32),
                        pltpu.VMEM((2, nc, 8, kw), F32)],
        compiler_params=pltpu.CompilerParams(
            dimension_semantics=("arbitrary",), vmem_limit_bytes=VMEM_LIMIT),
        name="gla",
    )(xf, *[_warg(p[n]) for n in names])


def _xa_kv_kernel(mem_ref, g_ref, wkv_ref, k_ref, v_ref, *, width):
    m = _rms(mem_ref[0], g_ref[...]).astype(BF16)
    kv = _dot(m, wkv_ref[...])
    k_ref[0] = kv[:, :width].astype(BF16)
    v_ref[0] = kv[:, width:].astype(BF16)


def _xa_kv(mem, g, wkv):
    b, m, d = mem.shape
    width = wkv.shape[1] // 2
    blk = lambda w: pl.BlockSpec((1, m, w), lambda i: (i, 0, 0))
    return pl.pallas_call(
        functools.partial(_xa_kv_kernel, width=width),
        grid=(b,),
        in_specs=[blk(d), _const_spec(g.shape), _wspec(wkv)],
        out_specs=(blk(width), blk(width)),
        out_shape=(jax.ShapeDtypeStruct((b, m, width), BF16),) * 2,
        compiler_params=pltpu.CompilerParams(dimension_semantics=("arbitrary",), vmem_limit_bytes=VMEM_LIMIT),
        name="xa_kv",
    )(mem, g, _warg(wkv))


def _xa_heads(h, wq_ref, k_ref, v_ref, hd):
    q = _dot(h, wq_ref[...]) * (float(hd) ** -0.5)
    outs = []
    for i in range(q.shape[1] // hd):
        hs = slice(i * hd, (i + 1) * hd)
        sc = _dot(q[:, hs].astype(BF16), k_ref[0, :, hs], NT)
        sc = sc - jnp.max(sc, axis=-1, keepdims=True)
        e = jnp.exp(sc)
        pr = e / jnp.sum(e, axis=-1, keepdims=True)
        outs.append(_dot(pr.astype(BF16), v_ref[0, :, hs]).astype(BF16))
    return jnp.concatenate(outs, axis=1)


def _merge_kernel(x_ref, h_ref, yr_ref, yg_ref, k_ref, v_ref, wq_ref, wgate_ref, wb_ref, wo_ref, o_ref, *, hd):
    x = x_ref[...]
    h = h_ref[...]
    d = x.shape[-1]
    yx = _xa_heads(h, wq_ref, k_ref, v_ref, hd)
    merged = None
    for j in range(3):
        gate = _sigmoid(_dot(h, wgate_ref[:, j * d:(j + 1) * d]))
        t = _dot((yr_ref[...], yg_ref[...], yx)[j], wb_ref[j]) * gate
        merged = t if merged is None else merged + t
    o_ref[...] = x + _dot(merged.astype(BF16), wo_ref[...])


def _merge(x, h, yr, yg, k, v, wq, wgate, wb, wo, *, seq, tm=512, hd=128):
    t, d = x.shape
    bw = yr.shape[1]
    m, width = k.shape[1], k.shape[2]
    row = lambda w: pl.BlockSpec((tm, w), lambda i: (i, 0))
    kvs = pl.BlockSpec((1, m, width), lambda i: ((i * tm) // seq, 0, 0))
    return pl.pallas_call(
        functools.partial(_merge_kernel, hd=hd),
        grid=(t // tm,),
        in_specs=[row(d), row(d), row(bw), row(bw), kvs, kvs, _wspec(wq), _wspec(wgate), _wspec(wb), _wspec(wo)],
        out_specs=row(d),
        out_shape=jax.ShapeDtypeStruct((t, d), F32),
        compiler_params=pltpu.CompilerParams(dimension_semantics=("arbitrary",), vmem_limit_bytes=VMEM_LIMIT),
        name="merge",
    )(x, h, yr, yg, k, v, _warg(wq), _warg(wgate), _warg(wb), _warg(wo))


def _block_ones(width, block):
    i = jnp.arange(width) // block
    return (i[:, None] == i[None, :]).astype(BF16)


def kernel(x, mem, ffn1_norm, ffn1_w_in, ffn1_w_out, mix_norm, mem_norm, w_in, rwkv_mu, rwkv_w0, rwkv_w2, rwkv_a0, rwkv_a2, rwkv_g2, rwkv_k_k, rwkv_k_a, rwkv_r_k, rwkv_ln_w, rwkv_ln_b, rwkv_v0, rwkv_v1, rwkv_v2, gla_conv, gla_a_up, gla_a_bias, gla_norm, xa_w_kv, w_branch, w_out, ffn2_norm, ffn2_w_in, ffn2_w_out, final_norm):
    b, s, d = x.shape
    depth = w_in.shape[0]
    rw = rwkv_w0.shape[1]
    dl, al_ = rwkv_w2.shape[1], rwkv_a2.shape[1]
    gl_ = rwkv_g2.shape[1]
    kw = gla_a_up.shape[2]
    vw = gla_norm.shape[1]
    glora = gla_a_up.shape[1]
    xw = xa_w_kv.shape[2] // 2
    rcols = 3 * rw + dl + al_ + gl_
    gcols = 2 * kw + vw + glora + vw
    assert dl + al_ == LANES and gl_ == LANES and dl == HEAD

    row1 = lambda a: a[:, None, :]
    w_r = w_in[:, :, :rcols].astype(BF16)
    g0 = rcols
    w_gqkv = w_in[:, :, g0:g0 + 2 * kw + vw]
    w_gal = w_in[:, :, g0 + 2 * kw + vw:g0 + 2 * kw + vw + glora]
    w_ggo = w_in[:, :, g0 + 2 * kw + vw + glora:g0 + gcols]
    w_g = jnp.concatenate([w_gqkv, w_ggo, jnp.pad(w_gal, ((0, 0), (0, 0), (0, LANES - glora)))], axis=-1).astype(BF16)
    w_xq = w_in[:, :, g0 + gcols:g0 + gcols + xw].astype(BF16)
    w_gate = w_in[:, :, g0 + gcols + xw:].astype(BF16)
    zl = jnp.zeros((depth, dl, rw), F32)
    w_lora = jnp.concatenate([jnp.concatenate([rwkv_w2, zl], axis=-1),
                              jnp.concatenate([zl, rwkv_a2], axis=-1)], axis=1).astype(BF16)
    b_lora = row1(jnp.concatenate([rwkv_w0, rwkv_a0], axis=-1))
    mv = rwkv_v1.shape[2]
    v1p = jnp.pad(rwkv_v1, ((0, 0), (0, 0), (0, LANES - mv))).astype(BF16)
    v2p = jnp.pad(rwkv_v2, ((0, 0), (0, LANES - mv), (0, 0))).astype(BF16)
    aup = jnp.pad(gla_a_up, ((0, 0), (0, LANES - glora), (0, 0))).astype(BF16)
    ones = _block_ones(2 * LANES, HEAD)
    f1i, f1o = ffn1_w_in.astype(BF16), ffn1_w_out.astype(BF16)
    f2i, f2o = ffn2_w_in.astype(BF16), ffn2_w_out.astype(BF16)
    wkv = xa_w_kv.astype(BF16)
    wb = w_branch.astype(BF16)
    wo = w_out.astype(BF16)
    g2 = rwkv_g2.astype(BF16)

    xf = x.reshape(b * s, d)
    v_first = None
    for l in range(depth):
        xf, hf = _ffn(xf, ffn1_norm[l][None, :], _Layer(f1i, l), _Layer(f1o, l), mix_norm[l][None, :], "emit")
        rp = dict(wr=_Layer(w_r, l), mu=rwkv_mu[l][None, :], wl=_Layer(w_lora, l), bl=b_lora[l],
                  g2=_Layer(g2, l), kkw=rwkv_k_k[l][None, :], ka=rwkv_k_a[l][None, :],
                  rk=rwkv_r_k[l].reshape(1, rw), lnw=rwkv_ln_w[l][None, :], lnb=rwkv_ln_b[l][None, :], ones=ones)
        if l == 0:
            y_r, v_first = _rwkv(hf, rp, None, None, seq=s)
        else:
            vmix = (rwkv_v0[l - 1][None, :], _Layer(v1p, l - 1), _Layer(v2p, l - 1))
            y_r = _rwkv(hf, rp, vmix, v_first, seq=s)
        gp = dict(wg=_Layer(w_g, l), conv=gla_conv[l], aup=_Layer(aup, l),
                  ab=gla_a_bias[l][None, :], nw=gla_norm[l][None, :])
        y_g = _gla(hf, gp, seq=s)
        k_m, v_m = _xa_kv(mem, mem_norm[l][None, :], _Layer(wkv, l))
        xf = _merge(xf, hf, y_r, y_g, k_m, v_m, _Layer(w_xq, l), _Layer(w_gate, l), _Layer(wb, l), _Layer(wo, l),
                    seq=s)
        if l == depth - 1:
            xf = _ffn(xf, ffn2_norm[l][None, :], _Layer(f2i, l), _Layer(f2o, l), final_norm[None, :], "final")
        else:
            xf = _ffn(xf, ffn2_norm[l][None, :], _Layer(f2i, l), _Layer(f2o, l))
    return xf.reshape(b, s, d)
```

```python
import functools

import jax
import jax.numpy as jnp
from jax import lax
from jax.experimental import pallas as pl
from jax.experimental.pallas import tpu as pltpu

F32 = jnp.float32
BF16 = jnp.bfloat16

NORM_EPS = 1e-6
RWKV_LN_EPS = 64e-5
GLA_TAU = 16.0
CHUNK = 64
HEAD = 64
LANES = 128
PACK = 128
GROUP = 4
VMEM_LIMIT = 56 * 1024 * 1024

NN = ((1,), (0,))
NT = ((1,), (1,))
TN = ((0,), (0,))


def _dot(a, b, dims=NN):
    return lax.dot_general(a, b, (dims, ((), ())), preferred_element_type=F32)


def _split(x, n):
    if x.dtype == BF16:
        return [x]
    parts, rest = [], x
    for i in range(n):
        p = rest.astype(BF16)
        parts.append(p)
        if i + 1 < n:
            rest = rest - p.astype(F32)
    return parts


def _mm(a, b, dims=NN, pa=1, pb=1):
    aa, bb = _split(a, pa), _split(b, pb)
    order = max(len(aa), len(bb))
    out = None
    for i, x in enumerate(aa):
        for j, y in enumerate(bb):
            if i + j < order:
                t = _dot(x, y, dims)
                out = t if out is None else out + t
    return out


def _sigmoid(x):
    return 1.0 / (1.0 + jnp.exp(-x))


def _softplus(x):
    return jnp.maximum(x, 0.0) + jnp.log(1.0 + jnp.exp(-jnp.abs(x)))


def _rms(x, g):
    return x * lax.rsqrt(jnp.mean(x * x, axis=-1, keepdims=True) + NORM_EPS) * g


def _stack_heads(x, head_lanes):
    x = x.astype(BF16)
    zero = jnp.zeros_like(x)
    return jnp.concatenate([jnp.where(m, x, zero) for m in head_lanes], axis=0)


def _head_lanes(width):
    lane = lax.broadcasted_iota(jnp.int32, (1, width), 1) // HEAD
    return [lane == h for h in range(width // HEAD)]


def _segsum(x, ones):
    w = ones.shape[0]
    xb = x.astype(BF16)
    return jnp.concatenate([_dot(xb[:, i:i + w], ones) for i in range(0, x.shape[1], w)], axis=1)


def _const_spec(shape):
    nd = len(shape)
    return pl.BlockSpec(shape, lambda *_: (0,) * nd)


class _Layer:
    def __init__(self, arr, l):
        self.arr, self.l, self.shape = arr, l, arr.shape[1:]


def _wspec(p):
    if isinstance(p, _Layer):
        l, nd = p.l, len(p.shape)
        return pl.BlockSpec((None,) + tuple(p.shape), lambda *_: (l,) + (0,) * nd, pipeline_mode=pl.Buffered(1))
    return _const_spec(p.shape)


def _warg(p):
    return p.arr if isinstance(p, _Layer) else p


def _ffn_kernel(x_ref, g_ref, win_ref, wout_ref, *rest, d_ff, tf, post):
    if post == "final":
        gf_ref, o_ref, acc_ref = rest
    elif post == "emit":
        gf_ref, o_ref, h_ref, acc_ref = rest
    else:
        o_ref, acc_ref = rest
    x = x_ref[...]
    h = _rms(x, g_ref[...]).astype(BF16)
    for c in range(d_ff // tf):
        gate = _dot(h, win_ref[:, c * tf:(c + 1) * tf])
        up = _dot(h, win_ref[:, d_ff + c * tf:d_ff + (c + 1) * tf])
        act = (gate * _sigmoid(gate) * up).astype(BF16)
        part = _dot(act, wout_ref[c * tf:(c + 1) * tf, :])
        if c == 0:
            acc_ref[...] = part
        else:
            acc_ref[...] += part
    y = x + 0.5 * acc_ref[...]
    if post == "final":
        y = _rms(y, gf_ref[...])
    elif post == "emit":
        h_ref[...] = _rms(y, gf_ref[...]).astype(BF16)
    o_ref[...] = y


def _ffn(x, g, w_in, w_out, post_g=None, post="none", *, tm=1024, tf=256):
    t, d = x.shape
    d_ff = w_out.shape[0]
    row = pl.BlockSpec((tm, d), lambda i: (i, 0))
    in_specs = [row, _const_spec((1, d)), _wspec(w_in), _wspec(w_out)]
    args = [x, g, _warg(w_in), _warg(w_out)]
    out_specs, out_shape = row, jax.ShapeDtypeStruct((t, d), F32)
    if post != "none":
        in_specs.append(_const_spec((1, d)))
        args.append(post_g)
    if post == "emit":
        out_specs, out_shape = (row, row), (out_shape, jax.ShapeDtypeStruct((t, d), BF16))
    return pl.pallas_call(
        functools.partial(_ffn_kernel, d_ff=d_ff, tf=tf, post=post),
        grid=(t // tm,),
        in_specs=in_specs,
        out_specs=out_specs,
        out_shape=out_shape,
        scratch_shapes=[pltpu.VMEM((tm, d), F32)],
        compiler_params=pltpu.CompilerParams(
            dimension_semantics=("arbitrary",), vmem_limit_bytes=VMEM_LIMIT),
        name="ffn_" + post,
    )(*args)


def _rwkv_phase1(probs, masks):
    lane_lo, strict, incl, levels, eye, blockdiag = masks
    n = range(len(probs))
    ar = [jnp.concatenate([q["at"], q["rt"]], axis=0).astype(BF16) for q in probs]
    ab = [_dot(ar[i], _stack_heads(probs[i]["bh"], lane_lo), NT) for i in n]
    yield
    ak = [_dot(ar[i], _stack_heads(probs[i]["kh"], lane_lo), NT) for i in n]
    a_ab = [jnp.where(strict, ab[i][:CHUNK], 0.0) for i in n]
    a_rb = [jnp.where(incl, ab[i][CHUNK:], 0.0).astype(BF16) for i in n]
    t = [eye + jnp.where(levels[0], a_ab[i], 0.0) for i in n]
    yield
    for m in levels[1:]:
        x = [_dot(t[i].astype(BF16), _stack_heads(jnp.where(m, a_ab[i], 0.0), lane_lo)) for i in n]
        yield
        t = [t[i] + _dot(x[i].astype(BF16), _stack_heads(t[i], lane_lo)) for i in n]
        yield
    a_ak = [jnp.where(strict, ak[i][:CHUNK], 0.0).astype(BF16) for i in n]
    a_rk = [jnp.where(incl, ak[i][CHUNK:], 0.0).astype(BF16) for i in n]
    vst = [_stack_heads(q["v"], lane_lo) for q in probs]
    av = [_dot(a_ak[i], vst[i]) for i in n]
    arkv = [_dot(a_rk[i], vst[i]) for i in n]
    yield
    vk = [jnp.where(blockdiag, _dot(probs[i]["v"].astype(BF16), probs[i]["kbar"].astype(BF16), TN), 0.0)
          for i in n]
    for i, q in enumerate(probs):
        q["store"](ar=ar[i], t=t[i].astype(BF16), arb=a_rb[i], bbar=q["bbar"].astype(BF16),
                   av=av[i], arkv=arkv[i], vk=vk[i])
    yield


def _interleave(*gens):
    live = list(gens)
    while live:
        for g in list(live):
            try:
                next(g)
            except StopIteration:
                live.remove(g)


def _pack_masks():
    row = lax.broadcasted_iota(jnp.int32, (CHUNK, PACK), 0)
    col = lax.broadcasted_iota(jnp.int32, (CHUNK, PACK), 1) % HEAD
    strict = row > col
    incl = row >= col
    levels = []
    k = 0
    while (1 << k) < CHUNK:
        levels.append(((row >> (k + 1)) == (col >> (k + 1)))
                      & (((row >> k) & 1) == 1) & (((col >> k) & 1) == 0))
        k += 1
    eye = (row == col).astype(F32)
    r2 = lax.broadcasted_iota(jnp.int32, (PACK, PACK), 0) // HEAD
    c2 = lax.broadcasted_iota(jnp.int32, (PACK, PACK), 1) // HEAD
    blockdiag = r2 == c2
    return _head_lanes(PACK), strict, incl, levels, eye, blockdiag


def _tri_ones():
    r = lax.broadcasted_iota(jnp.int32, (CHUNK, CHUNK), 0)
    c = lax.broadcasted_iota(jnp.int32, (CHUNK, CHUNK), 1)
    return (r >= c).astype(BF16)


def _rwkv_kernel(*refs, ts, width, has_vmix, tiles_per_seq, n_tiles):
    (h_ref, wr_ref, mu_ref, wl_ref, bl_ref, g2_ref, kkw_ref, ka_ref, rk_ref,
     lnw_ref, lnb_ref, ones_ref) = refs[:12]
    refs = refs[12:]
    if has_vmix:
        v0_ref, v1_ref, v2_ref, vf_ref = refs[:4]
        refs = refs[4:]
        y_ref = refs[0]
        refs = refs[1:]
    else:
        y_ref, vf_out_ref = refs[:2]
        refs = refs[2:]
    (uprev, st, r_s, k_s, v_s, kk_s, b_s, lw_s, g_s, bg_s, y_s,
     ar_s, t_s, arb_s, bbar_s, av_s, arkv_s, vk_s, egc_s) = refs
    n_pairs = width // PACK
    step = pl.program_id(0)
    cur = step % 2
    prv = 1 - cur

    @pl.when(step == 0)
    def _():
        for ref in (g_s, bg_s, ar_s, t_s, arb_s, bbar_s, av_s, arkv_s, vk_s, egc_s):
            ref[1] = jnp.zeros(ref.shape[1:], ref.dtype)
        if not has_vmix:
            vf_out_ref[...] = jnp.zeros_like(vf_out_ref)

    @pl.when(step % tiles_per_seq == 0)
    def _():
        uprev[...] = jnp.zeros_like(uprev)

    @pl.when((step + tiles_per_seq - 1) % tiles_per_seq == 0)
    def _():
        st[...] = jnp.zeros_like(st)

    ones = ones_ref[...]

    def pre():
        u = _dot(h_ref[...], wr_ref[...])
        yield
        rolled = pltpu.roll(u, 1, 0)
        first = jnp.where(lax.broadcasted_iota(jnp.int32, (8, 1), 0) == 0, uprev[0:1, :], rolled[:8])
        u_prev = jnp.concatenate([first, rolled[8:]], axis=0)
        uprev[...] = pltpu.roll(u[ts - 8:ts, :], 1, 0)
        us = u + mu_ref[...] * (u_prev - u)
        r = us[:, 0:width]
        k = us[:, width:2 * width]
        v = us[:, 2 * width:3 * width]
        wa = us[:, 3 * width:3 * width + LANES]
        gl = us[:, 3 * width + LANES:3 * width + 2 * LANES]
        if has_vmix:
            z = _mm(_mm(v, v1_ref[...]), v2_ref[...])
            v = v + (vf_ref[...] - v) * _sigmoid(v0_ref[...] + z)
        else:
            vf_out_ref[...] = jnp.where(step < n_tiles, v, vf_out_ref[...])
        yield
        lane = lax.broadcasted_iota(jnp.int32, (1, LANES), 1)
        act = jnp.where(lane < HEAD, jnp.tanh(wa), wa)
        lo = _mm(act, wl_ref[...]) + bl_ref[...]
        w_log = -_softplus(-lo[:, :width]) - 0.5
        lw_s[...] = -jnp.exp(w_log)
        a = _sigmoid(lo[:, width:])
        yield
        g = _mm(_sigmoid(gl), g2_ref[...])
        kk = k * kkw_ref[...]
        ss = _segsum(kk * kk, ones)
        kk = kk * jnp.minimum(lax.rsqrt(ss), 1e12)
        yield
        kh = k * (1.0 + (a - 1.0) * ka_ref[...])
        bonus = _segsum(r * kh * rk_ref[...], ones) * v
        r_s[...] = r
        k_s[...] = kh
        v_s[...] = v
        kk_s[...] = kk
        b_s[...] = kk * a
        g_s[cur] = g
        bg_s[cur] = bonus * g
        yield

    masks = _pack_masks()
    tri = _tri_ones()

    lane_lo, blockdiag = masks[0], masks[5]
    n_groups = ts // (CHUNK * GROUP)
    inv_n = 1.0 / HEAD

    def store_to(c, p):
        def store(ar, t, arb, bbar, av, arkv, vk):
            ar_s[cur, c, p] = ar
            t_s[cur, c, p] = t
            arb_s[cur, c, p] = arb
            bbar_s[cur, c, p] = bbar
            av_s[cur, c, p] = av
            arkv_s[cur, c, p] = arkv
            vk_s[cur, c, p] = vk
        return store

    def phase1(gi):
        probs = []
        for j in range(GROUP):
            c = gi * GROUP + j
            sl = slice(c * CHUNK, (c + 1) * CHUNK)
            lwc = lw_s[sl, :]
            gc = _mm(tri, lwc, NN, 1, 3)
            e_g = jnp.exp(gc)
            e_ng = jnp.exp(-gc)
            e_gm = jnp.exp(gc - lwc)
            e_gc = jnp.exp(gc[CHUNK - 1:CHUNK, :])
            egc_s[cur, c] = jnp.broadcast_to(e_gc, (8, width))
            rt = r_s[sl, :] * e_g
            at = -kk_s[sl, :] * e_gm
            bh = b_s[sl, :] * e_ng
            kh_ = k_s[sl, :] * e_ng
            bbar = bh * e_gc
            kbar = kh_ * e_gc
            vc = v_s[sl, :]
            for p in range(n_pairs):
                ls = slice(p * PACK, (p + 1) * PACK)
                probs.append(dict(rt=rt[:, ls], at=at[:, ls], bh=bh[:, ls], kh=kh_[:, ls],
                                  bbar=bbar[:, ls], kbar=kbar[:, ls], v=vc[:, ls], store=store_to(c, p)))
        yield from _rwkv_phase1(probs, masks)

    def phase2(gi):
        pairs = range(n_pairs)
        for c in range(gi * GROUP, (gi + 1) * GROUP):
            sl = slice(c * CHUNK, (c + 1) * CHUNK)
            s = [st[p] for p in pairs]
            wy = [_dot(ar_s[prv, c, p], s[p].astype(BF16), NT) for p in pairs]
            yield
            u = [_dot(t_s[prv, c, p], _stack_heads(wy[p][:CHUNK] + av_s[prv, c, p], lane_lo)) for p in pairs]
            yield
            y = [wy[p][CHUNK:] + _dot(arb_s[prv, c, p], _stack_heads(u[p], lane_lo)) + arkv_s[prv, c, p]
                 for p in pairs]
            e_gc = egc_s[prv, c][0:1, :]
            for p in pairs:
                ls = slice(p * PACK, (p + 1) * PACK)
                upd = _dot(u[p].astype(BF16), bbar_s[prv, c, p], TN)
                st[p] = s[p] * e_gc[:, ls] + jnp.where(blockdiag, upd, 0.0) + vk_s[prv, c, p]
                y_s[sl, ls] = y[p]
            yield

    def post(gi):
        sl = slice(gi * GROUP * CHUNK, (gi + 1) * GROUP * CHUNK)
        y = y_s[sl, :]
        mean = _segsum(y, ones) * inv_n
        yield
        d = y - mean
        var = _segsum(d * d, ones) * inv_n
        yn = d * lax.rsqrt(var + RWKV_LN_EPS) * lnw_ref[...] + lnb_ref[...]
        y_ref[sl, :] = (yn * g_s[prv, sl, :] + bg_s[prv, sl, :]).astype(BF16)
        yield

    def stream_a():
        yield from pre()
        for gi in range(n_groups):
            yield from phase1(gi)

    def stream_b():
        for gi in range(n_groups):
            yield from phase2(gi)
            yield from post(gi)

    _interleave(stream_a(), stream_b())


def _lag_specs(n_tiles, ts):
    cur = lambda w: pl.BlockSpec((ts, w), lambda i: (jnp.minimum(i, n_tiles - 1), 0))
    lag = lambda w: pl.BlockSpec((ts, w), lambda i: (jnp.maximum(i - 1, 0), 0))
    return cur, lag


def _rwkv(xf, p, vmix, v_first, *, seq, ts=512):
    t, d = xf.shape
    width = p["kkw"].shape[-1]
    has_vmix = vmix is not None
    n_tiles = t // ts
    cur, lag = _lag_specs(n_tiles, ts)
    names = ("wr", "mu", "wl", "bl", "g2", "kkw", "ka", "rk", "lnw", "lnb", "ones")
    args = [xf] + [_warg(p[n]) for n in names]
    in_specs = [cur(d)] + [_wspec(p[n]) for n in names]
    if has_vmix:
        args += [_warg(a) for a in vmix] + [v_first]
        in_specs += [_wspec(a) for a in vmix] + [cur(width)]
        out_shape = jax.ShapeDtypeStruct((t, width), BF16)
        out_specs = lag(width)
    else:
        out_shape = (jax.ShapeDtypeStruct((t, width), BF16), jax.ShapeDtypeStruct((t, width), F32))
        out_specs = (lag(width), cur(width))
    ucols = p["wr"].shape[1]
    tw = pltpu.VMEM((ts, width), F32)
    tw2 = pltpu.VMEM((2, ts, width), F32)
    nc, npairs = ts // CHUNK, width // PACK
    per = lambda rows, dt: pltpu.VMEM((2, nc, npairs, rows, PACK), dt)
    scratch = ([pltpu.VMEM((8, ucols), F32), pltpu.VMEM((npairs, PACK, PACK), F32)] + [tw] * 6 + [tw2, tw2, tw]
               + [per(2 * CHUNK, BF16), per(CHUNK, BF16), per(CHUNK, BF16), per(CHUNK, BF16),
                  per(CHUNK, F32), per(CHUNK, F32), per(PACK, F32), pltpu.VMEM((2, nc, 8, width), F32)])
    return pl.pallas_call(
        functools.partial(_rwkv_kernel, ts=ts, width=width, has_vmix=has_vmix, tiles_per_seq=seq // ts,
                          n_tiles=n_tiles),
        grid=(n_tiles + 1,),
        in_specs=in_specs,
        out_specs=out_specs,
        out_shape=out_shape,
        scratch_shapes=scratch,
        compiler_params=pltpu.CompilerParams(
            dimension_semantics=("arbitrary",), vmem_limit_bytes=VMEM_LIMIT),
        name="rwkv_vmix" if has_vmix else "rwkv",
    )(*args)


def _gla_kernel(h_ref, wg_ref, conv_ref, aup_ref, ab_ref, nw_ref, y_ref,
                carry, st, q_s, k_s, v_s, la_s, o_s, go_s, qg_s, kv_s, el_s,
                *, ts, kw, vw, n_heads, tiles_per_seq):
    dv = vw // n_heads
    qkv_w = 2 * kw + vw
    step = pl.program_id(0)
    cur = step % 2
    prv = 1 - cur

    @pl.when(step == 0)
    def _():
        for ref in (o_s, go_s, qg_s, kv_s, el_s):
            ref[1] = jnp.zeros(ref.shape[1:], ref.dtype)

    @pl.when(step % tiles_per_seq == 0)
    def _():
        carry[...] = jnp.zeros_like(carry)

    @pl.when((step + tiles_per_seq - 1) % tiles_per_seq == 0)
    def _():
        st[...] = jnp.zeros_like(st)

    def pre_tile():
        u = _dot(h_ref[...], wg_ref[...])
        yield
        qkv = u[:, :qkv_w]
        go = u[:, qkv_w:qkv_w + vw]
        al = u[:, qkv_w + vw:qkv_w + vw + LANES]
        go_s[cur] = go * _sigmoid(go)
        rows = lax.broadcasted_iota(jnp.int32, (ts, 1), 0)
        prev8 = carry[...]
        taps = conv_ref.shape[0]
        out = qkv * conv_ref[taps - 1:taps, :]
        for j in range(1, taps):
            shifted = pltpu.roll(qkv, j, 0)
            fill = pltpu.roll(prev8, j, 0)
            first = jnp.where(rows[:8] < j, fill, shifted[:8])
            shifted = jnp.concatenate([first, shifted[8:]], axis=0)
            out = out + shifted * conv_ref[taps - 1 - j:taps - j, :]
            yield
        carry[...] = qkv[ts - 8:ts, :]
        qkv = out * _sigmoid(out)
        log_a = _mm(al, aup_ref[...]) + ab_ref[...]
        log_a = (jnp.minimum(log_a, 0.0) - jnp.log(1.0 + jnp.exp(-jnp.abs(log_a)))) * (1.0 / GLA_TAU)
        q_s[...] = qkv[:, :kw] * (float(kw // n_heads) ** -0.5)
        k_s[...] = qkv[:, kw:2 * kw]
        v_s[...] = qkv[:, 2 * kw:].astype(BF16)
        la_s[...] = log_a
        yield

    pair_lanes = _head_lanes(LANES)
    lane_lo = pair_lanes[0]
    r = lax.broadcasted_iota(jnp.int32, (CHUNK, CHUNK), 0)
    c = lax.broadcasted_iota(jnp.int32, (CHUNK, CHUNK), 1)
    lower = r >= c
    tri = lower.astype(BF16)
    n_groups = ts // (CHUNK * GROUP)
    n_kpairs = kw // LANES

    def phase1(gi):
        chunks = range(gi * GROUP, (gi + 1) * GROUP)
        pre = {}
        for ci in chunks:
            sl = slice(ci * CHUNK, (ci + 1) * CHUNK)
            gc = _mm(tri, la_s[sl, :], NN, 1, 3)
            e_g = jnp.exp(gc)
            e_ng = jnp.exp(-gc)
            e_last = jnp.exp(gc[CHUNK - 1:CHUNK, :])
            el_s[cur, ci] = jnp.broadcast_to(e_last, (8, kw))
            q = q_s[sl, :]
            k = k_s[sl, :]
            qg = q * e_g
            kg = k * e_ng
            qg_s[cur, ci] = qg.astype(BF16)
            pre[ci] = (qg, q * e_ng, kg.astype(BF16), (k * e_g).astype(BF16), (kg * e_last).astype(BF16))
        yield
        a_past = {(ci, pr): _dot(_stack_heads(pre[ci][0][:, pr * LANES:(pr + 1) * LANES], pair_lanes),
                                 pre[ci][2][:, pr * LANES:(pr + 1) * LANES], NT)
                  for ci in chunks for pr in range(n_kpairs)}
        yield
        a_fut = {(ci, pr): _dot(_stack_heads(pre[ci][1][:, pr * LANES:(pr + 1) * LANES], pair_lanes),
                                pre[ci][3][:, pr * LANES:(pr + 1) * LANES], NT)
                 for ci in chunks for pr in range(n_kpairs)}
        yield
        zero = jnp.zeros((CHUNK, LANES), BF16)
        for ci in chunks:
            sl = slice(ci * CHUNK, (ci + 1) * CHUNK)
            for hd in range(n_heads):
                pr, hh = divmod(hd, 2)
                hs = slice(hd * dv, (hd + 1) * dv)
                rs = slice(hh * CHUNK, (hh + 1) * CHUNK)
                a = jnp.where(lower, a_past[ci, pr][rs], a_fut[ci, pr][rs]).astype(BF16)
                vh = v_s[sl, hs]
                o_s[cur, sl, hs] = _dot(a, vh)
                kb = pre[ci][4][:, pr * LANES:(pr + 1) * LANES]
                kb = jnp.where(lane_lo, kb, zero) if hh == 0 else jnp.where(lane_lo, zero, kb)
                kv_s[cur, ci, hd] = _dot(vh, kb, TN)
            yield

    def phase2(gi):
        for ci in range(gi * GROUP, (gi + 1) * GROUP):
            sl = slice(ci * CHUNK, (ci + 1) * CHUNK)
            e_last = el_s[prv, ci][0:1, :]
            for hd in range(n_heads):
                pr = hd // 2
                ls = slice(pr * LANES, (pr + 1) * LANES)
                hs = slice(hd * dv, (hd + 1) * dv)
                s_h = st[hd]
                o_s[prv, sl, hs] += _dot(qg_s[prv, ci][:, ls], s_h.astype(BF16), NT)
                st[hd] = s_h * e_last[:, ls] + kv_s[prv, ci, hd]
            yield

    def post(gi):
        sl = slice(gi * GROUP * CHUNK, (gi + 1) * GROUP * CHUNK)
        for hd in range(n_heads):
            hs = slice(hd * dv, (hd + 1) * dv)
            o = o_s[prv, sl, hs]
            o = o * lax.rsqrt(jnp.mean(o * o, axis=-1, keepdims=True) + NORM_EPS) * nw_ref[:, hs]
            y_ref[sl, hs] = (o * go_s[prv, sl, hs]).astype(BF16)
            yield

    def stream_a():
        yield from pre_tile()
        for gi in range(n_groups):
            yield from phase1(gi)

    def stream_b():
        for gi in range(n_groups):
            yield from phase2(gi)
            yield from post(gi)

    _interleave(stream_a(), stream_b())


def _gla(xf, p, *, seq, ts=512):
    t, d = xf.shape
    kw = p["aup"].shape[1]
    vw = p["nw"].shape[1]
    n_heads = kw // HEAD
    n_tiles = t // ts
    cur, lag = _lag_specs(n_tiles, ts)
    names = ("wg", "conv", "aup", "ab", "nw")
    tkw = pltpu.VMEM((ts, kw), F32)
    tvw2 = pltpu.VMEM((2, ts, vw), F32)
    nc = ts // CHUNK
    return pl.pallas_call(
        functools.partial(_gla_kernel, ts=ts, kw=kw, vw=vw, n_heads=n_heads, tiles_per_seq=seq // ts),
        grid=(n_tiles + 1,),
        in_specs=[cur(d)] + [_wspec(p[n]) for n in names],
        out_specs=lag(vw),
        out_shape=jax.ShapeDtypeStruct((t, vw), BF16),
        scratch_shapes=[pltpu.VMEM((8, 2 * kw + vw), F32), pltpu.VMEM((n_heads, vw // n_heads, LANES), F32),
                        tkw, tkw, pltpu.VMEM((ts, vw), BF16), tkw, tvw2, tvw2,
                        pltpu.VMEM((2, nc, CHUNK, kw), BF16),
                        pltpu.VMEM((2, nc, n_heads, vw // n_heads, LANES), F32),
                        pltpu.VMEM((2, nc, 8, kw), F32)],
        compiler_params=pltpu.CompilerParams(
            dimension_semantics=("arbitrary",), vmem_limit_bytes=VMEM_LIMIT),
        name="gla",
    )(xf, *[_warg(p[n]) for n in names])


def _xa_kv_kernel(mem_ref, g_ref, wkv_ref, k_ref, v_ref, *, width):
    m = _rms(mem_ref[0], g_ref[...]).astype(BF16)
    kv = _dot(m, wkv_ref[...])
    k_ref[0] = kv[:, :width].astype(BF16)
    v_ref[0] = kv[:, width:].astype(BF16)


def _xa_kv(mem, g, wkv):
    b, m, d = mem.shape
    width = wkv.shape[1] // 2
    blk = lambda w: pl.BlockSpec((1, m, w), lambda i: (i, 0, 0))
    return pl.pallas_call(
        functools.partial(_xa_kv_kernel, width=width),
        grid=(b,),
        in_specs=[blk(d), _const_spec(g.shape), _wspec(wkv)],
        out_specs=(blk(width), blk(width)),
        out_shape=(jax.ShapeDtypeStruct((b, m, width), BF16),) * 2,
        compiler_params=pltpu.CompilerParams(dimension_semantics=("arbitrary",), vmem_limit_bytes=VMEM_LIMIT),
        name="xa_kv",
    )(mem, g, _warg(wkv))


def _xa_heads(h, wq_ref, k_ref, v_ref, hd):
    q = _dot(h, wq_ref[...]) * (float(hd) ** -0.5)
    outs = []
    for i in range(q.shape[1] // hd):
        hs = slice(i * hd, (i + 1) * hd)
        sc = _dot(q[:, hs].astype(BF16), k_ref[0, :, hs], NT)
        sc = sc - jnp.max(sc, axis=-1, keepdims=True)
        e = jnp.exp(sc)
        pr = e / jnp.sum(e, axis=-1, keepdims=True)
        outs.append(_dot(pr.astype(BF16), v_ref[0, :, hs]).astype(BF16))
    return jnp.concatenate(outs, axis=1)


def _merge_kernel(x_ref, h_ref, yr_ref, yg_ref, k_ref, v_ref, wq_ref, wgate_ref, wb_ref, wo_ref, o_ref, *, hd):
    x = x_ref[...]
    h = h_ref[...]
    d = x.shape[-1]
    yx = _xa_heads(h, wq_ref, k_ref, v_ref, hd)
    merged = None
    for j in range(3):
        gate = _sigmoid(_dot(h, wgate_ref[:, j * d:(j + 1) * d]))
        t = _dot((yr_ref[...], yg_ref[...], yx)[j], wb_ref[j]) * gate
        merged = t if merged is None else merged + t
    o_ref[...] = x + _dot(merged.astype(BF16), wo_ref[...])


def _merge(x, h, yr, yg, k, v, wq, wgate, wb, wo, *, seq, tm=1024, hd=128):
    t, d = x.shape
    bw = yr.shape[1]
    m, width = k.shape[1], k.shape[2]
    row = lambda w: pl.BlockSpec((tm, w), lambda i: (i, 0))
    kvs = pl.BlockSpec((1, m, width), lambda i: ((i * tm) // seq, 0, 0))
    return pl.pallas_call(
        functools.partial(_merge_kernel, hd=hd),
        grid=(t // tm,),
        in_specs=[row(d), row(d), row(bw), row(bw), kvs, kvs, _wspec(wq), _wspec(wgate), _wspec(wb), _wspec(wo)],
        out_specs=row(d),
        out_shape=jax.ShapeDtypeStruct((t, d), F32),
        compiler_params=pltpu.CompilerParams(dimension_semantics=("arbitrary",), vmem_limit_bytes=VMEM_LIMIT),
        name="merge",
    )(x, h, yr, yg, k, v, _warg(wq), _warg(wgate), _warg(wb), _warg(wo))


def _block_ones(width, block):
    i = jnp.arange(width) // block
    return (i[:, None] == i[None, :]).astype(BF16)


def kernel(x, mem, ffn1_norm, ffn1_w_in, ffn1_w_out, mix_norm, mem_norm, w_in, rwkv_mu, rwkv_w0, rwkv_w2, rwkv_a0, rwkv_a2, rwkv_g2, rwkv_k_k, rwkv_k_a, rwkv_r_k, rwkv_ln_w, rwkv_ln_b, rwkv_v0, rwkv_v1, rwkv_v2, gla_conv, gla_a_up, gla_a_bias, gla_norm, xa_w_kv, w_branch, w_out, ffn2_norm, ffn2_w_in, ffn2_w_out, final_norm):
    b, s, d = x.shape
    depth = w_in.shape[0]
    rw = rwkv_w0.shape[1]
    dl, al_ = rwkv_w2.shape[1], rwkv_a2.shape[1]
    gl_ = rwkv_g2.shape[1]
    kw = gla_a_up.shape[2]
    vw = gla_norm.shape[1]
    glora = gla_a_up.shape[1]
    xw = xa_w_kv.shape[2] // 2
    rcols = 3 * rw + dl + al_ + gl_
    gcols = 2 * kw + vw + glora + vw
    assert dl + al_ == LANES and gl_ == LANES and dl == HEAD

    row1 = lambda a: a[:, None, :]
    w_r = w_in[:, :, :rcols].astype(BF16)
    g0 = rcols
    w_gqkv = w_in[:, :, g0:g0 + 2 * kw + vw]
    w_gal = w_in[:, :, g0 + 2 * kw + vw:g0 + 2 * kw + vw + glora]
    w_ggo = w_in[:, :, g0 + 2 * kw + vw + glora:g0 + gcols]
    w_g = jnp.concatenate([w_gqkv, w_ggo, jnp.pad(w_gal, ((0, 0), (0, 0), (0, LANES - glora)))], axis=-1).astype(BF16)
    w_xq = w_in[:, :, g0 + gcols:g0 + gcols + xw].astype(BF16)
    w_gate = w_in[:, :, g0 + gcols + xw:].astype(BF16)
    zl = jnp.zeros((depth, dl, rw), F32)
    w_lora = jnp.concatenate([jnp.concatenate([rwkv_w2, zl], axis=-1),
                              jnp.concatenate([zl, rwkv_a2], axis=-1)], axis=1).astype(BF16)
    b_lora = row1(jnp.concatenate([rwkv_w0, rwkv_a0], axis=-1))
    mv = rwkv_v1.shape[2]
    v1p = jnp.pad(rwkv_v1, ((0, 0), (0, 0), (0, LANES - mv))).astype(BF16)
    v2p = jnp.pad(rwkv_v2, ((0, 0), (0, LANES - mv), (0, 0))).astype(BF16)
    aup = jnp.pad(gla_a_up, ((0, 0), (0, LANES - glora), (0, 0))).astype(BF16)
    ones = _block_ones(2 * LANES, HEAD)
    f1i, f1o = ffn1_w_in.astype(BF16), ffn1_w_out.astype(BF16)
    f2i, f2o = ffn2_w_in.astype(BF16), ffn2_w_out.astype(BF16)
    wkv = xa_w_kv.astype(BF16)
    wb = w_branch.astype(BF16)
    wo = w_out.astype(BF16)
    g2 = rwkv_g2.astype(BF16)

    xf = x.reshape(b * s, d)
    v_first = None
    for l in range(depth):
        xf, hf = _ffn(xf, ffn1_norm[l][None, :], _Layer(f1i, l), _Layer(f1o, l), mix_norm[l][None, :], "emit")
        rp = dict(wr=_Layer(w_r, l), mu=rwkv_mu[l][None, :], wl=_Layer(w_lora, l), bl=b_lora[l],
                  g2=_Layer(g2, l), kkw=rwkv_k_k[l][None, :], ka=rwkv_k_a[l][None, :],
                  rk=rwkv_r_k[l].reshape(1, rw), lnw=rwkv_ln_w[l][None, :], lnb=rwkv_ln_b[l][None, :], ones=ones)
        if l == 0:
            y_r, v_first = _rwkv(hf, rp, None, None, seq=s)
        else:
            vmix = (rwkv_v0[l - 1][None, :], _Layer(v1p, l - 1), _Layer(v2p, l - 1))
            y_r = _rwkv(hf, rp, vmix, v_first, seq=s)
        gp = dict(wg=_Layer(w_g, l), conv=gla_conv[l], aup=_Layer(aup, l),
                  ab=gla_a_bias[l][None, :], nw=gla_norm[l][None, :])
        y_g = _gla(hf, gp, seq=s)
        k_m, v_m = _xa_kv(mem, mem_norm[l][None, :], _Layer(wkv, l))
        xf = _merge(xf, hf, y_r, y_g, k_m, v_m, _Layer(w_xq, l), _Layer(w_gate, l), _Layer(wb, l), _Layer(wo, l),
                    seq=s)
        if l == depth - 1:
            xf = _ffn(xf, ffn2_norm[l][None, :], _Layer(f2i, l), _Layer(f2o, l), final_norm[None, :], "final")
        else:
            xf = _ffn(xf, ffn2_norm[l][None, :], _Layer(f2i, l), _Layer(f2o, l))
    return xf.reshape(b, s, d)
```

```python
import functools
import math

import jax
import jax.numpy as jnp
from jax import lax
from jax.experimental import pallas as pl
from jax.experimental.pallas import tpu as pltpu

F32 = jnp.float32
BF16 = jnp.bfloat16

NORM_EPS = 1e-6
RWKV_LN_EPS = 64e-5
GLA_TAU = 16.0
CHUNK = 64
HEAD = 64
LANES = 128
PACK = 128
GROUP = 4
GLA_GROUP = 8
DECAY_SCALE = -math.exp(-0.5)
VMEM_LIMIT = 56 * 1024 * 1024

NN = ((1,), (0,))
NT = ((1,), (1,))
TN = ((0,), (0,))


def _dot(a, b, dims=NN):
    return lax.dot_general(a, b, (dims, ((), ())), preferred_element_type=F32)


def _split(x, n):
    if x.dtype == BF16:
        return [x]
    parts, rest = [], x
    for i in range(n):
        p = rest.astype(BF16)
        parts.append(p)
        if i + 1 < n:
            rest = rest - p.astype(F32)
    return parts


def _mm(a, b, dims=NN, pa=1, pb=1):
    aa, bb = _split(a, pa), _split(b, pb)
    order = max(len(aa), len(bb))
    out = None
    for i, x in enumerate(aa):
        for j, y in enumerate(bb):
            if i + j < order:
                t = _dot(x, y, dims)
                out = t if out is None else out + t
    return out


def _sigmoid(x):
    return 1.0 / (1.0 + jnp.exp(-x))


def _rms(x, g):
    return x * lax.rsqrt(jnp.mean(x * x, axis=-1, keepdims=True) + NORM_EPS) * g


def _stack_heads(x, head_lanes):
    x = x.astype(BF16)
    zero = jnp.zeros_like(x)
    return jnp.concatenate([jnp.where(m, x, zero) for m in head_lanes], axis=0)


def _head_lanes(width):
    lane = lax.broadcasted_iota(jnp.int32, (1, width), 1) // HEAD
    return [lane == h for h in range(width // HEAD)]


def _segsum(x, ones):
    w = ones.shape[0]
    xb = x.astype(BF16)
    return jnp.concatenate([_dot(xb[:, i:i + w], ones) for i in range(0, x.shape[1], w)], axis=1)


def _const_spec(shape):
    nd = len(shape)
    return pl.BlockSpec(shape, lambda *_: (0,) * nd)


class _Layer:
    def __init__(self, arr, l):
        self.arr, self.l, self.shape = arr, l, arr.shape[1:]


def _wspec(p):
    if isinstance(p, _Layer):
        l, nd = p.l, len(p.shape)
        return pl.BlockSpec((None,) + tuple(p.shape), lambda *_: (l,) + (0,) * nd, pipeline_mode=pl.Buffered(1))
    return _const_spec(p.shape)


def _warg(p):
    return p.arr if isinstance(p, _Layer) else p


def _ffn_kernel(x_ref, g_ref, win_ref, wout_ref, *rest, d_ff, tf, post):
    if post == "final":
        gf_ref, o_ref, acc_ref = rest
    elif post == "emit":
        gf_ref, o_ref, h_ref, acc_ref = rest
    else:
        o_ref, acc_ref = rest
    x = x_ref[...]
    h = _rms(x, g_ref[...]).astype(BF16)
    for c in range(d_ff // tf):
        gate = _dot(h, win_ref[:, c * tf:(c + 1) * tf])
        up = _dot(h, win_ref[:, d_ff + c * tf:d_ff + (c + 1) * tf])
        acc_ref[:, c * tf:(c + 1) * tf] = (gate * _sigmoid(gate) * up).astype(BF16)
    y = x + 0.5 * _dot(acc_ref[...], wout_ref[...])
    if post == "final":
        y = _rms(y, gf_ref[...])
    elif post == "emit":
        h_ref[...] = _rms(y, gf_ref[...]).astype(BF16)
    o_ref[...] = y


def _ffn(x, g, w_in, w_out, post_g=None, post="none", *, tm=1024, tf=256):
    t, d = x.shape
    d_ff = w_out.shape[0]
    row = pl.BlockSpec((tm, d), lambda i: (i, 0))
    in_specs = [row, _const_spec((1, d)), _wspec(w_in), _wspec(w_out)]
    args = [x, g, _warg(w_in), _warg(w_out)]
    out_specs, out_shape = row, jax.ShapeDtypeStruct((t, d), F32)
    if post != "none":
        in_specs.append(_const_spec((1, d)))
        args.append(post_g)
    if post == "emit":
        out_specs, out_shape = (row, row), (out_shape, jax.ShapeDtypeStruct((t, d), BF16))
    return pl.pallas_call(
        functools.partial(_ffn_kernel, d_ff=d_ff, tf=tf, post=post),
        grid=(t // tm,),
        in_specs=in_specs,
        out_specs=out_specs,
        out_shape=out_shape,
        scratch_shapes=[pltpu.VMEM((tm, d_ff), BF16)],
        compiler_params=pltpu.CompilerParams(
            dimension_semantics=("arbitrary",), vmem_limit_bytes=VMEM_LIMIT),
        name="ffn_" + post,
    )(*args)


def _rwkv_phase1(probs, masks):
    lane_lo, strict, incl, levels, eye, blockdiag = masks
    n = range(len(probs))
    ar = [jnp.concatenate([q["at"], q["rt"]], axis=0).astype(BF16) for q in probs]
    ab = [_dot(ar[i], _stack_heads(probs[i]["bh"], lane_lo), NT) for i in n]
    yield
    ak = [_dot(ar[i], _stack_heads(probs[i]["kh"], lane_lo), NT) for i in n]
    a_ab = [jnp.where(strict, ab[i][:CHUNK], 0.0) for i in n]
    a_rb = [jnp.where(incl, ab[i][CHUNK:], 0.0).astype(BF16) for i in n]
    t = [eye + jnp.where(levels[0], a_ab[i], 0.0) for i in n]
    yield
    for m in levels[1:]:
        x = [_dot(t[i].astype(BF16), _stack_heads(jnp.where(m, a_ab[i], 0.0), lane_lo)) for i in n]
        yield
        t = [t[i] + _dot(x[i].astype(BF16), _stack_heads(t[i], lane_lo)) for i in n]
        yield
    a_ak = [jnp.where(strict, ak[i][:CHUNK], 0.0).astype(BF16) for i in n]
    a_rk = [jnp.where(incl, ak[i][CHUNK:], 0.0).astype(BF16) for i in n]
    vst = [_stack_heads(q["v"], lane_lo) for q in probs]
    av = [_dot(a_ak[i], vst[i]) for i in n]
    arkv = [_dot(a_rk[i], vst[i]) for i in n]
    yield
    vk = [jnp.where(blockdiag, _dot(probs[i]["v"].astype(BF16), probs[i]["kbar"].astype(BF16), TN), 0.0)
          for i in n]
    for i, q in enumerate(probs):
        q["store"](ar=ar[i], t=t[i].astype(BF16), arb=a_rb[i], bbar=q["bbar"].astype(BF16),
                   av=av[i], arkv=arkv[i], vk=vk[i])
    yield


def _interleave(*gens):
    live = list(gens)
    while live:
        for g in list(live):
            try:
                next(g)
            except StopIteration:
                live.remove(g)


def _pack_masks():
    row = lax.broadcasted_iota(jnp.int32, (CHUNK, PACK), 0)
    col = lax.broadcasted_iota(jnp.int32, (CHUNK, PACK), 1) % HEAD
    strict = row > col
    incl = row >= col
    levels = []
    k = 0
    while (1 << k) < CHUNK:
        levels.append(((row >> (k + 1)) == (col >> (k + 1)))
                      & (((row >> k) & 1) == 1) & (((col >> k) & 1) == 0))
        k += 1
    eye = (row == col).astype(F32)
    r2 = lax.broadcasted_iota(jnp.int32, (PACK, PACK), 0) // HEAD
    c2 = lax.broadcasted_iota(jnp.int32, (PACK, PACK), 1) // HEAD
    blockdiag = r2 == c2
    return _head_lanes(PACK), strict, incl, levels, eye, blockdiag


def _tri_ones():
    r = lax.broadcasted_iota(jnp.int32, (CHUNK, CHUNK), 0)
    c = lax.broadcasted_iota(jnp.int32, (CHUNK, CHUNK), 1)
    return (r >= c).astype(BF16)


def _rwkv_kernel(*refs, ts, width, has_vmix, tiles_per_seq, n_tiles):
    (h_ref, wr_ref, mu_ref, wl_ref, bl_ref, g2_ref, kkw_ref, ka_ref, rk_ref,
     lnw_ref, lnb_ref, ones_ref) = refs[:12]
    refs = refs[12:]
    if has_vmix:
        v0_ref, v1_ref, v2_ref, vf_ref = refs[:4]
        refs = refs[4:]
        y_ref = refs[0]
        refs = refs[1:]
    else:
        y_ref, vf_out_ref = refs[:2]
        refs = refs[2:]
    (uprev, st, r_s, k_s, v_s, kk_s, b_s, lw_s, g_s, bg_s, y_s,
     ar_s, t_s, arb_s, bbar_s, av_s, arkv_s, vk_s, egc_s) = refs
    n_pairs = width // PACK
    step = pl.program_id(0)
    cur = step % 2
    prv = 1 - cur

    @pl.when(step == 0)
    def _():
        for ref in (g_s, bg_s, ar_s, t_s, arb_s, bbar_s, av_s, arkv_s, vk_s, egc_s):
            ref[1] = jnp.zeros(ref.shape[1:], ref.dtype)
        if not has_vmix:
            vf_out_ref[...] = jnp.zeros_like(vf_out_ref)

    @pl.when(step % tiles_per_seq == 0)
    def _():
        uprev[...] = jnp.zeros_like(uprev)

    @pl.when((step + tiles_per_seq - 1) % tiles_per_seq == 0)
    def _():
        st[...] = jnp.zeros_like(st)

    ones = ones_ref[...]

    def pre():
        u = _dot(h_ref[...], wr_ref[...])
        yield
        rolled = pltpu.roll(u, 1, 0)
        first = jnp.where(lax.broadcasted_iota(jnp.int32, (8, 1), 0) == 0, uprev[0:1, :], rolled[:8])
        u_prev = jnp.concatenate([first, rolled[8:]], axis=0)
        uprev[...] = pltpu.roll(u[ts - 8:ts, :], 1, 0)
        us = u + mu_ref[...] * (u_prev - u)
        r = us[:, 0:width]
        k = us[:, width:2 * width]
        v = us[:, 2 * width:3 * width]
        wa = us[:, 3 * width:3 * width + LANES]
        gl = us[:, 3 * width + LANES:3 * width + 2 * LANES]
        if has_vmix:
            z = _mm(_mm(v, v1_ref[...]), v2_ref[...])
            v = v + (vf_ref[...] - v) * _sigmoid(v0_ref[...] + z)
        else:
            vf_out_ref[...] = jnp.where(step < n_tiles, v, vf_out_ref[...])
        yield
        lane = lax.broadcasted_iota(jnp.int32, (1, LANES), 1)
        act = jnp.where(lane < HEAD, jnp.tanh(wa), wa)
        lo = _mm(act, wl_ref[...]) + bl_ref[...]
        lw_s[...] = DECAY_SCALE * _sigmoid(lo[:, :width])
        a = _sigmoid(lo[:, width:])
        yield
        g = _mm(_sigmoid(gl), g2_ref[...])
        kk = k * kkw_ref[...]
        ss = _segsum(kk * kk, ones)
        kk = kk * jnp.minimum(lax.rsqrt(ss), 1e12)
        yield
        kh = k * (1.0 + (a - 1.0) * ka_ref[...])
        bonus = _segsum(r * kh * rk_ref[...], ones) * v
        r_s[...] = r
        k_s[...] = kh
        v_s[...] = v
        kk_s[...] = kk
        b_s[...] = kk * a
        g_s[cur] = g
        bg_s[cur] = bonus * g
        yield

    masks = _pack_masks()
    tri = _tri_ones()

    lane_lo, blockdiag = masks[0], masks[5]
    n_groups = ts // (CHUNK * GROUP)
    inv_n = 1.0 / HEAD

    def store_to(c, p):
        def store(ar, t, arb, bbar, av, arkv, vk):
            ar_s[cur, c, p] = ar
            t_s[cur, c, p] = t
            arb_s[cur, c, p] = arb
            bbar_s[cur, c, p] = bbar
            av_s[cur, c, p] = av
            arkv_s[cur, c, p] = arkv
            vk_s[cur, c, p] = vk
        return store

    def phase1(gi):
        probs = []
        for j in range(GROUP):
            c = gi * GROUP + j
            sl = slice(c * CHUNK, (c + 1) * CHUNK)
            lwc = lw_s[sl, :]
            gc = _mm(tri, lwc, NN, 1, 3)
            e_g = jnp.exp(gc)
            e_ng = jnp.exp(-gc)
            e_gm = jnp.exp(gc - lwc)
            e_gc = jnp.exp(gc[CHUNK - 1:CHUNK, :])
            egc_s[cur, c] = jnp.broadcast_to(e_gc, (8, width))
            rt = r_s[sl, :] * e_g
            at = -kk_s[sl, :] * e_gm
            bh = b_s[sl, :] * e_ng
            kh_ = k_s[sl, :] * e_ng
            bbar = bh * e_gc
            kbar = kh_ * e_gc
            vc = v_s[sl, :]
            for p in range(n_pairs):
                ls = slice(p * PACK, (p + 1) * PACK)
                probs.append(dict(rt=rt[:, ls], at=at[:, ls], bh=bh[:, ls], kh=kh_[:, ls],
                                  bbar=bbar[:, ls], kbar=kbar[:, ls], v=vc[:, ls], store=store_to(c, p)))
        yield from _rwkv_phase1(probs, masks)

    def phase2(gi):
        pairs = range(n_pairs)
        for c in range(gi * GROUP, (gi + 1) * GROUP):
            sl = slice(c * CHUNK, (c + 1) * CHUNK)
            s = [st[p] for p in pairs]
            wy = [_dot(ar_s[prv, c, p], s[p].astype(BF16), NT) for p in pairs]
            yield
            u = [_dot(t_s[prv, c, p], _stack_heads(wy[p][:CHUNK] + av_s[prv, c, p], lane_lo)) for p in pairs]
            yield
            y = [wy[p][CHUNK:] + _dot(arb_s[prv, c, p], _stack_heads(u[p], lane_lo)) + arkv_s[prv, c, p]
                 for p in pairs]
            e_gc = egc_s[prv, c][0:1, :]
            for p in pairs:
                ls = slice(p * PACK, (p + 1) * PACK)
                upd = _dot(u[p].astype(BF16), bbar_s[prv, c, p], TN)
                st[p] = s[p] * e_gc[:, ls] + jnp.where(blockdiag, upd, 0.0) + vk_s[prv, c, p]
                y_s[sl, ls] = y[p]
            yield

    def post(gi):
        sl = slice(gi * GROUP * CHUNK, (gi + 1) * GROUP * CHUNK)
        y = y_s[sl, :]
        mean = _segsum(y, ones) * inv_n
        yield
        d = y - mean
        var = _segsum(d * d, ones) * inv_n
        yn = d * lax.rsqrt(var + RWKV_LN_EPS) * lnw_ref[...] + lnb_ref[...]
        y_ref[sl, :] = (yn * g_s[prv, sl, :] + bg_s[prv, sl, :]).astype(BF16)
        yield

    def stream_a():
        yield from pre()
        for gi in range(n_groups):
            yield from phase1(gi)

    def stream_b():
        for gi in range(n_groups):
            yield from phase2(gi)
            yield from post(gi)

    _interleave(stream_a(), stream_b())


def _lag_specs(n_tiles, ts):
    cur = lambda w: pl.BlockSpec((ts, w), lambda i: (jnp.minimum(i, n_tiles - 1), 0))
    lag = lambda w: pl.BlockSpec((ts, w), lambda i: (jnp.maximum(i - 1, 0), 0))
    return cur, lag


def _rwkv(xf, p, vmix, v_first, *, seq, ts=512):
    t, d = xf.shape
    width = p["kkw"].shape[-1]
    has_vmix = vmix is not None
    n_tiles = t // ts
    cur, lag = _lag_specs(n_tiles, ts)
    names = ("wr", "mu", "wl", "bl", "g2", "kkw", "ka", "rk", "lnw", "lnb", "ones")
    args = [xf] + [_warg(p[n]) for n in names]
    in_specs = [cur(d)] + [_wspec(p[n]) for n in names]
    if has_vmix:
        args += [_warg(a) for a in vmix] + [v_first]
        in_specs += [_wspec(a) for a in vmix] + [cur(width)]
        out_shape = jax.ShapeDtypeStruct((t, width), BF16)
        out_specs = lag(width)
    else:
        out_shape = (jax.ShapeDtypeStruct((t, width), BF16), jax.ShapeDtypeStruct((t, width), F32))
        out_specs = (lag(width), cur(width))
    ucols = p["wr"].shape[1]
    tw = pltpu.VMEM((ts, width), F32)
    tw2 = pltpu.VMEM((2, ts, width), F32)
    nc, npairs = ts // CHUNK, width // PACK
    per = lambda rows, dt: pltpu.VMEM((2, nc, npairs, rows, PACK), dt)
    scratch = ([pltpu.VMEM((8, ucols), F32), pltpu.VMEM((npairs, PACK, PACK), F32)] + [tw] * 6 + [tw2, tw2, tw]
               + [per(2 * CHUNK, BF16), per(CHUNK, BF16), per(CHUNK, BF16), per(CHUNK, BF16),
                  per(CHUNK, F32), per(CHUNK, F32), per(PACK, F32), pltpu.VMEM((2, nc, 8, width), F32)])
    return pl.pallas_call(
        functools.partial(_rwkv_kernel, ts=ts, width=width, has_vmix=has_vmix, tiles_per_seq=seq // ts,
                          n_tiles=n_tiles),
        grid=(n_tiles + 1,),
        in_specs=in_specs,
        out_specs=out_specs,
        out_shape=out_shape,
        scratch_shapes=scratch,
        compiler_params=pltpu.CompilerParams(
            dimension_semantics=("arbitrary",), vmem_limit_bytes=VMEM_LIMIT),
        name="rwkv_vmix" if has_vmix else "rwkv",
    )(*args)


def _gla_kernel(h_ref, wg_ref, conv_ref, aup_ref, ab_ref, nw_ref, y_ref,
                carry, st, q_s, k_s, v_s, la_s, o_s, go_s, qg_s, kv_s, el_s,
                *, ts, kw, vw, n_heads, tiles_per_seq):
    dv = vw // n_heads
    qkv_w = 2 * kw + vw
    step = pl.program_id(0)
    cur = step % 2
    prv = 1 - cur

    @pl.when(step == 0)
    def _():
        for ref in (o_s, go_s, qg_s, kv_s, el_s):
            ref[1] = jnp.zeros(ref.shape[1:], ref.dtype)

    @pl.when(step % tiles_per_seq == 0)
    def _():
        carry[...] = jnp.zeros_like(carry)

    @pl.when((step + tiles_per_seq - 1) % tiles_per_seq == 0)
    def _():
        st[...] = jnp.zeros_like(st)

    def pre_tile():
        u = _dot(h_ref[...], wg_ref[...])
        yield
        qkv = u[:, :qkv_w]
        go = u[:, qkv_w:qkv_w + vw]
        al = u[:, qkv_w + vw:qkv_w + vw + LANES]
        go_s[cur] = go * _sigmoid(go)
        rows = lax.broadcasted_iota(jnp.int32, (ts, 1), 0)
        prev8 = carry[...]
        taps = conv_ref.shape[0]
        out = qkv * conv_ref[taps - 1:taps, :]
        for j in range(1, taps):
            shifted = pltpu.roll(qkv, j, 0)
            fill = pltpu.roll(prev8, j, 0)
            first = jnp.where(rows[:8] < j, fill, shifted[:8])
            shifted = jnp.concatenate([first, shifted[8:]], axis=0)
            out = out + shifted * conv_ref[taps - 1 - j:taps - j, :]
            yield
        carry[...] = qkv[ts - 8:ts, :]
        qkv = out * _sigmoid(out)
        log_a = _mm(al, aup_ref[...]) + ab_ref[...]
        log_a = (jnp.minimum(log_a, 0.0) - jnp.log(1.0 + jnp.exp(-jnp.abs(log_a)))) * (1.0 / GLA_TAU)
        q_s[...] = qkv[:, :kw] * (float(kw // n_heads) ** -0.5)
        k_s[...] = qkv[:, kw:2 * kw]
        v_s[...] = qkv[:, 2 * kw:].astype(BF16)
        la_s[...] = log_a
        yield

    pair_lanes = _head_lanes(LANES)
    lane_lo = pair_lanes[0]
    r = lax.broadcasted_iota(jnp.int32, (CHUNK, CHUNK), 0)
    c = lax.broadcasted_iota(jnp.int32, (CHUNK, CHUNK), 1)
    lower = r >= c
    tri = lower.astype(BF16)
    n_groups = ts // (CHUNK * GLA_GROUP)
    n_kpairs = kw // LANES

    def phase1(gi):
        chunks = range(gi * GLA_GROUP, (gi + 1) * GLA_GROUP)
        pre = {}
        for ci in chunks:
            sl = slice(ci * CHUNK, (ci + 1) * CHUNK)
            gc = _mm(tri, la_s[sl, :], NN, 1, 3)
            e_g = jnp.exp(gc)
            e_ng = jnp.exp(-gc)
            e_last = jnp.exp(gc[CHUNK - 1:CHUNK, :])
            el_s[cur, ci] = jnp.broadcast_to(e_last, (8, kw))
            q = q_s[sl, :]
            k = k_s[sl, :]
            qg = q * e_g
            kg = k * e_ng
            qg_s[cur, ci] = qg.astype(BF16)
            pre[ci] = (qg, q * e_ng, kg.astype(BF16), (k * e_g).astype(BF16), (kg * e_last).astype(BF16))
        yield
        a_past = {(ci, pr): _dot(_stack_heads(pre[ci][0][:, pr * LANES:(pr + 1) * LANES], pair_lanes),
                                 pre[ci][2][:, pr * LANES:(pr + 1) * LANES], NT)
                  for ci in chunks for pr in range(n_kpairs)}
        yield
        a_fut = {(ci, pr): _dot(_stack_heads(pre[ci][1][:, pr * LANES:(pr + 1) * LANES], pair_lanes),
                                pre[ci][3][:, pr * LANES:(pr + 1) * LANES], NT)
                 for ci in chunks for pr in range(n_kpairs)}
        yield
        zero = jnp.zeros((CHUNK, LANES), BF16)
        for ci in chunks:
            sl = slice(ci * CHUNK, (ci + 1) * CHUNK)
            for hd in range(n_heads):
                pr, hh = divmod(hd, 2)
                hs = slice(hd * dv, (hd + 1) * dv)
                rs = slice(hh * CHUNK, (hh + 1) * CHUNK)
                a = jnp.where(lower, a_past[ci, pr][rs], a_fut[ci, pr][rs]).astype(BF16)
                vh = v_s[sl, hs]
                o_s[cur, sl, hs] = _dot(a, vh)
                kb = pre[ci][4][:, pr * LANES:(pr + 1) * LANES]
                kb = jnp.where(lane_lo, kb, zero) if hh == 0 else jnp.where(lane_lo, zero, kb)
                kv_s[cur, ci, hd] = _dot(vh, kb, TN)
            yield

    def phase2(gi):
        for ci in range(gi * GLA_GROUP, (gi + 1) * GLA_GROUP):
            sl = slice(ci * CHUNK, (ci + 1) * CHUNK)
            e_last = el_s[prv, ci][0:1, :]
            for hd in range(n_heads):
                pr = hd // 2
                ls = slice(pr * LANES, (pr + 1) * LANES)
                hs = slice(hd * dv, (hd + 1) * dv)
                s_h = st[hd]
                o_s[prv, sl, hs] += _dot(qg_s[prv, ci][:, ls], s_h.astype(BF16), NT)
                st[hd] = s_h * e_last[:, ls] + kv_s[prv, ci, hd]
            yield

    def post(gi):
        sl = slice(gi * GLA_GROUP * CHUNK, (gi + 1) * GLA_GROUP * CHUNK)
        for hd in range(n_heads):
            hs = slice(hd * dv, (hd + 1) * dv)
            o = o_s[prv, sl, hs]
            o = o * lax.rsqrt(jnp.mean(o * o, axis=-1, keepdims=True) + NORM_EPS) * nw_ref[:, hs]
            y_ref[sl, hs] = (o * go_s[prv, sl, hs]).astype(BF16)
            yield

    def stream_a():
        yield from pre_tile()
        for gi in range(n_groups):
            yield from phase1(gi)

    def stream_b():
        for gi in range(n_groups):
            yield from phase2(gi)
            yield from post(gi)

    _interleave(stream_a(), stream_b())


def _gla(xf, p, *, seq, ts=512):
    t, d = xf.shape
    kw = p["aup"].shape[1]
    vw = p["nw"].shape[1]
    n_heads = kw // HEAD
    n_tiles = t // ts
    cur, lag = _lag_specs(n_tiles, ts)
    names = ("wg", "conv", "aup", "ab", "nw")
    tkw = pltpu.VMEM((ts, kw), F32)
    tvw2 = pltpu.VMEM((2, ts, vw), F32)
    nc = ts // CHUNK
    return pl.pallas_call(
        functools.partial(_gla_kernel, ts=ts, kw=kw, vw=vw, n_heads=n_heads, tiles_per_seq=seq // ts),
        grid=(n_tiles + 1,),
        in_specs=[cur(d)] + [_wspec(p[n]) for n in names],
        out_specs=lag(vw),
        out_shape=jax.ShapeDtypeStruct((t, vw), BF16),
        scratch_shapes=[pltpu.VMEM((8, 2 * kw + vw), F32), pltpu.VMEM((n_heads, vw // n_heads, LANES), F32),
                        tkw, tkw, pltpu.VMEM((ts, vw), BF16), tkw, tvw2, tvw2,
                        pltpu.VMEM((2, nc, CHUNK, kw), BF16),
                        pltpu.VMEM((2, nc, n_heads, vw // n_heads, LANES), F32),
                        pltpu.VMEM((2, nc, 8, kw), F32)],
        compiler_params=pltpu.CompilerParams(
            dimension_semantics=("arbitrary",), vmem_limit_bytes=VMEM_LIMIT),
        name="gla",
    )(xf, *[_warg(p[n]) for n in names])


def _xa_kv_kernel(mem_ref, g_ref, wkv_ref, k_ref, v_ref, *, width):
    m = _rms(mem_ref[0], g_ref[...]).astype(BF16)
    kv = _dot(m, wkv_ref[...])
    k_ref[0] = kv[:, :width].astype(BF16)
    v_ref[0] = kv[:, width:].astype(BF16)


def _xa_kv(mem, g, wkv):
    b, m, d = mem.shape
    width = wkv.shape[1] // 2
    blk = lambda w: pl.BlockSpec((1, m, w), lambda i: (i, 0, 0))
    return pl.pallas_call(
        functools.partial(_xa_kv_kernel, width=width),
        grid=(b,),
        in_specs=[blk(d), _const_spec(g.shape), _wspec(wkv)],
        out_specs=(blk(width), blk(width)),
        out_shape=(jax.ShapeDtypeStruct((b, m, width), BF16),) * 2,
        compiler_params=pltpu.CompilerParams(dimension_semantics=("arbitrary",), vmem_limit_bytes=VMEM_LIMIT),
        name="xa_kv",
    )(mem, g, _warg(wkv))


def _xa_heads(h, wq_ref, k_ref, v_ref, hd):
    q = _dot(h, wq_ref[...]) * (float(hd) ** -0.5)
    outs = []
    for i in range(q.shape[1] // hd):
        hs = slice(i * hd, (i + 1) * hd)
        sc = _dot(q[:, hs].astype(BF16), k_ref[0, :, hs], NT)
        sc = sc - jnp.max(sc, axis=-1, keepdims=True)
        e = jnp.exp(sc)
        pr = e / jnp.sum(e, axis=-1, keepdims=True)
        outs.append(_dot(pr.astype(BF16), v_ref[0, :, hs]).astype(BF16))
    return jnp.concatenate(outs, axis=1)


def _merge_kernel(x_ref, h_ref, yr_ref, yg_ref, k_ref, v_ref, wq_ref, wgate_ref, wb_ref, wo_ref, o_ref, *, hd):
    x = x_ref[...]
    h = h_ref[...]
    d = x.shape[-1]
    yx = _xa_heads(h, wq_ref, k_ref, v_ref, hd)
    merged = None
    for j in range(3):
        gate = _sigmoid(_dot(h, wgate_ref[:, j * d:(j + 1) * d]))
        t = _dot((yr_ref[...], yg_ref[...], yx)[j], wb_ref[j]) * gate
        merged = t if merged is None else merged + t
    o_ref[...] = x + _dot(merged.astype(BF16), wo_ref[...])


def _merge(x, h, yr, yg, k, v, wq, wgate, wb, wo, *, seq, tm=1024, hd=128):
    t, d = x.shape
    bw = yr.shape[1]
    m, width = k.shape[1], k.shape[2]
    row = lambda w: pl.BlockSpec((tm, w), lambda i: (i, 0))
    kvs = pl.BlockSpec((1, m, width), lambda i: ((i * tm) // seq, 0, 0))
    return pl.pallas_call(
        functools.partial(_merge_kernel, hd=hd),
        grid=(t // tm,),
        in_specs=[row(d), row(d), row(bw), row(bw), kvs, kvs, _wspec(wq), _wspec(wgate), _wspec(wb), _wspec(wo)],
        out_specs=row(d),
        out_shape=jax.ShapeDtypeStruct((t, d), F32),
        compiler_params=pltpu.CompilerParams(dimension_semantics=("arbitrary",), vmem_limit_bytes=VMEM_LIMIT),
        name="merge",
    )(x, h, yr, yg, k, v, _warg(wq), _warg(wgate), _warg(wb), _warg(wo))


def _block_ones(width, block):
    i = jnp.arange(width) // block
    return (i[:, None] == i[None, :]).astype(BF16)


def kernel(x, mem, ffn1_norm, ffn1_w_in, ffn1_w_out, mix_norm, mem_norm, w_in, rwkv_mu, rwkv_w0, rwkv_w2, rwkv_a0, rwkv_a2, rwkv_g2, rwkv_k_k, rwkv_k_a, rwkv_r_k, rwkv_ln_w, rwkv_ln_b, rwkv_v0, rwkv_v1, rwkv_v2, gla_conv, gla_a_up, gla_a_bias, gla_norm, xa_w_kv, w_branch, w_out, ffn2_norm, ffn2_w_in, ffn2_w_out, final_norm):
    b, s, d = x.shape
    depth = w_in.shape[0]
    rw = rwkv_w0.shape[1]
    dl, al_ = rwkv_w2.shape[1], rwkv_a2.shape[1]
    gl_ = rwkv_g2.shape[1]
    kw = gla_a_up.shape[2]
    vw = gla_norm.shape[1]
    glora = gla_a_up.shape[1]
    xw = xa_w_kv.shape[2] // 2
    rcols = 3 * rw + dl + al_ + gl_
    gcols = 2 * kw + vw + glora + vw
    assert dl + al_ == LANES and gl_ == LANES and dl == HEAD

    row1 = lambda a: a[:, None, :]
    w_r = w_in[:, :, :rcols].astype(BF16)
    g0 = rcols
    w_gqkv = w_in[:, :, g0:g0 + 2 * kw + vw]
    w_gal = w_in[:, :, g0 + 2 * kw + vw:g0 + 2 * kw + vw + glora]
    w_ggo = w_in[:, :, g0 + 2 * kw + vw + glora:g0 + gcols]
    w_g = jnp.concatenate([w_gqkv, w_ggo, jnp.pad(w_gal, ((0, 0), (0, 0), (0, LANES - glora)))], axis=-1).astype(BF16)
    w_xq = w_in[:, :, g0 + gcols:g0 + gcols + xw].astype(BF16)
    w_gate = w_in[:, :, g0 + gcols + xw:].astype(BF16)
    zl = jnp.zeros((depth, dl, rw), F32)
    w_lora = jnp.concatenate([jnp.concatenate([rwkv_w2, zl], axis=-1),
                              jnp.concatenate([zl, rwkv_a2], axis=-1)], axis=1).astype(BF16)
    b_lora = row1(jnp.concatenate([rwkv_w0, rwkv_a0], axis=-1))
    mv = rwkv_v1.shape[2]
    v1p = jnp.pad(rwkv_v1, ((0, 0), (0, 0), (0, LANES - mv))).astype(BF16)
    v2p = jnp.pad(rwkv_v2, ((0, 0), (0, LANES - mv), (0, 0))).astype(BF16)
    aup = jnp.pad(gla_a_up, ((0, 0), (0, LANES - glora), (0, 0))).astype(BF16)
    ones = _block_ones(2 * LANES, HEAD)
    f1i, f1o = ffn1_w_in.astype(BF16), ffn1_w_out.astype(BF16)
    f2i, f2o = ffn2_w_in.astype(BF16), ffn2_w_out.astype(BF16)
    wkv = xa_w_kv.astype(BF16)
    wb = w_branch.astype(BF16)
    wo = w_out.astype(BF16)
    g2 = rwkv_g2.astype(BF16)

    xf = x.reshape(b * s, d)
    v_first = None
    for l in range(depth):
        xf, hf = _ffn(xf, ffn1_norm[l][None, :], _Layer(f1i, l), _Layer(f1o, l), mix_norm[l][None, :], "emit")
        rp = dict(wr=_Layer(w_r, l), mu=rwkv_mu[l][None, :], wl=_Layer(w_lora, l), bl=b_lora[l],
                  g2=_Layer(g2, l), kkw=rwkv_k_k[l][None, :], ka=rwkv_k_a[l][None, :],
                  rk=rwkv_r_k[l].reshape(1, rw), lnw=rwkv_ln_w[l][None, :], lnb=rwkv_ln_b[l][None, :], ones=ones)
        if l == 0:
            y_r, v_first = _rwkv(hf, rp, None, None, seq=s)
        else:
            vmix = (rwkv_v0[l - 1][None, :], _Layer(v1p, l - 1), _Layer(v2p, l - 1))
            y_r = _rwkv(hf, rp, vmix, v_first, seq=s)
        gp = dict(wg=_Layer(w_g, l), conv=gla_conv[l], aup=_Layer(aup, l),
                  ab=gla_a_bias[l][None, :], nw=gla_norm[l][None, :])
        y_g = _gla(hf, gp, seq=s)
        k_m, v_m = _xa_kv(mem, mem_norm[l][None, :], _Layer(wkv, l))
        xf = _merge(xf, hf, y_r, y_g, k_m, v_m, _Layer(w_xq, l), _Layer(w_gate, l), _Layer(wb, l), _Layer(wo, l),
                    seq=s)
        if l == depth - 1:
            xf = _ffn(xf, ffn2_norm[l][None, :], _Layer(f2i, l), _Layer(f2o, l), final_norm[None, :], "final")
        else:
            xf = _ffn(xf, ffn2_norm[l][None, :], _Layer(f2i, l), _Layer(f2o, l))
    return xf.reshape(b, s, d)
```

```python
import functools
import math

import jax
import jax.numpy as jnp
from jax import lax
from jax.experimental import pallas as pl
from jax.experimental.pallas import tpu as pltpu

F32 = jnp.float32
BF16 = jnp.bfloat16

NORM_EPS = 1e-6
RWKV_LN_EPS = 64e-5
GLA_TAU = 16.0
CHUNK = 64
HEAD = 64
LANES = 128
PACK = 128
GROUP = 4
GLA_GROUP = 8
DECAY_SCALE = -math.exp(-0.5)
VMEM_LIMIT = 56 * 1024 * 1024

NN = ((1,), (0,))
NT = ((1,), (1,))
TN = ((0,), (0,))


def _dot(a, b, dims=NN):
    return lax.dot_general(a, b, (dims, ((), ())), preferred_element_type=F32)


def _split(x, n):
    if x.dtype == BF16:
        return [x]
    parts, rest = [], x
    for i in range(n):
        p = rest.astype(BF16)
        parts.append(p)
        if i + 1 < n:
            rest = rest - p.astype(F32)
    return parts


def _mm(a, b, dims=NN, pa=1, pb=1):
    aa, bb = _split(a, pa), _split(b, pb)
    order = max(len(aa), len(bb))
    out = None
    for i, x in enumerate(aa):
        for j, y in enumerate(bb):
            if i + j < order:
                t = _dot(x, y, dims)
                out = t if out is None else out + t
    return out


def _sigmoid(x):
    return 1.0 / (1.0 + jnp.exp(-x))


def _rms(x, g):
    return x * lax.rsqrt(jnp.mean(x * x, axis=-1, keepdims=True) + NORM_EPS) * g


def _stack_heads(x, head_lanes):
    x = x.astype(BF16)
    zero = jnp.zeros_like(x)
    return jnp.concatenate([jnp.where(m, x, zero) for m in head_lanes], axis=0)


def _head_lanes(width):
    lane = lax.broadcasted_iota(jnp.int32, (1, width), 1) // HEAD
    return [lane == h for h in range(width // HEAD)]


def _segsum(x, ones):
    w = ones.shape[0]
    xb = x.astype(BF16)
    return jnp.concatenate([_dot(xb[:, i:i + w], ones) for i in range(0, x.shape[1], w)], axis=1)


def _const_spec(shape):
    nd = len(shape)
    return pl.BlockSpec(shape, lambda *_: (0,) * nd)


class _Layer:
    def __init__(self, arr, l):
        self.arr, self.l, self.shape = arr, l, arr.shape[1:]


def _wspec(p):
    if isinstance(p, _Layer):
        l, nd = p.l, len(p.shape)
        return pl.BlockSpec((None,) + tuple(p.shape), lambda *_: (l,) + (0,) * nd, pipeline_mode=pl.Buffered(1))
    return _const_spec(p.shape)


def _warg(p):
    return p.arr if isinstance(p, _Layer) else p


def _ffn_kernel(x_ref, g_ref, win_ref, wout_ref, *rest, d_ff, tf, post):
    if post == "final":
        gf_ref, o_ref, acc_ref = rest
    elif post == "emit":
        gf_ref, o_ref, h_ref, acc_ref = rest
    else:
        o_ref, acc_ref = rest
    x = x_ref[...]
    h = _rms(x, g_ref[...]).astype(BF16)
    for c in range(d_ff // tf):
        gate = _dot(h, win_ref[:, c * tf:(c + 1) * tf])
        up = _dot(h, win_ref[:, d_ff + c * tf:d_ff + (c + 1) * tf])
        acc_ref[:, c * tf:(c + 1) * tf] = (gate * _sigmoid(gate) * up).astype(BF16)
    y = x + 0.5 * _dot(acc_ref[...], wout_ref[...])
    if post == "final":
        y = _rms(y, gf_ref[...])
    elif post == "emit":
        h_ref[...] = _rms(y, gf_ref[...]).astype(BF16)
    o_ref[...] = y


def _ffn(x, g, w_in, w_out, post_g=None, post="none", *, tm=1024, tf=256):
    t, d = x.shape
    d_ff = w_out.shape[0]
    row = pl.BlockSpec((tm, d), lambda i: (i, 0))
    in_specs = [row, _const_spec((1, d)), _wspec(w_in), _wspec(w_out)]
    args = [x, g, _warg(w_in), _warg(w_out)]
    out_specs, out_shape = row, jax.ShapeDtypeStruct((t, d), F32)
    if post != "none":
        in_specs.append(_const_spec((1, d)))
        args.append(post_g)
    if post == "emit":
        out_specs, out_shape = (row, row), (out_shape, jax.ShapeDtypeStruct((t, d), BF16))
    return pl.pallas_call(
        functools.partial(_ffn_kernel, d_ff=d_ff, tf=tf, post=post),
        grid=(t // tm,),
        in_specs=in_specs,
        out_specs=out_specs,
        out_shape=out_shape,
        scratch_shapes=[pltpu.VMEM((tm, d_ff), BF16)],
        compiler_params=pltpu.CompilerParams(
            dimension_semantics=("arbitrary",), vmem_limit_bytes=VMEM_LIMIT),
        name="ffn_" + post,
    )(*args)


def _rwkv_phase1(probs, masks):
    lane_lo, strict, incl, levels, eye, blockdiag = masks
    n = range(len(probs))
    ar = [jnp.concatenate([q["at"], q["rt"]], axis=0).astype(BF16) for q in probs]
    ab = [_dot(ar[i], _stack_heads(probs[i]["bh"], lane_lo), NT) for i in n]
    yield
    ak = [_dot(ar[i], _stack_heads(probs[i]["kh"], lane_lo), NT) for i in n]
    a_ab = [jnp.where(strict, ab[i][:CHUNK], 0.0) for i in n]
    a_rb = [jnp.where(incl, ab[i][CHUNK:], 0.0).astype(BF16) for i in n]
    t = [eye + jnp.where(levels[0], a_ab[i], 0.0) for i in n]
    yield
    for m in levels[1:]:
        x = [_dot(t[i].astype(BF16), _stack_heads(jnp.where(m, a_ab[i], 0.0), lane_lo)) for i in n]
        yield
        t = [t[i] + _dot(x[i].astype(BF16), _stack_heads(t[i], lane_lo)) for i in n]
        yield
    a_ak = [jnp.where(strict, ak[i][:CHUNK], 0.0).astype(BF16) for i in n]
    a_rk = [jnp.where(incl, ak[i][CHUNK:], 0.0).astype(BF16) for i in n]
    vst = [_stack_heads(q["v"], lane_lo) for q in probs]
    av = [_dot(a_ak[i], vst[i]) for i in n]
    arkv = [_dot(a_rk[i], vst[i]) for i in n]
    yield
    vk = [jnp.where(blockdiag, _dot(probs[i]["v"].astype(BF16), probs[i]["kbar"].astype(BF16), TN), 0.0)
          for i in n]
    for i, q in enumerate(probs):
        q["store"](ar=ar[i], t=t[i].astype(BF16), arb=a_rb[i], bbar=q["bbar"].astype(BF16),
                   av=av[i], arkv=arkv[i], vk=vk[i])
    yield


def _interleave(*gens):
    live = list(gens)
    while live:
        for g in list(live):
            try:
                next(g)
            except StopIteration:
                live.remove(g)


def _pack_masks():
    row = lax.broadcasted_iota(jnp.int32, (CHUNK, PACK), 0)
    col = lax.broadcasted_iota(jnp.int32, (CHUNK, PACK), 1) % HEAD
    strict = row > col
    incl = row >= col
    levels = []
    k = 0
    while (1 << k) < CHUNK:
        levels.append(((row >> (k + 1)) == (col >> (k + 1)))
                      & (((row >> k) & 1) == 1) & (((col >> k) & 1) == 0))
        k += 1
    eye = (row == col).astype(F32)
    r2 = lax.broadcasted_iota(jnp.int32, (PACK, PACK), 0) // HEAD
    c2 = lax.broadcasted_iota(jnp.int32, (PACK, PACK), 1) // HEAD
    blockdiag = r2 == c2
    return _head_lanes(PACK), strict, incl, levels, eye, blockdiag


def _tri_ones():
    r = lax.broadcasted_iota(jnp.int32, (CHUNK, CHUNK), 0)
    c = lax.broadcasted_iota(jnp.int32, (CHUNK, CHUNK), 1)
    return (r >= c).astype(BF16)


def _rwkv_kernel(*refs, ts, width, has_vmix, tiles_per_seq, n_tiles, streams):
    (h_ref, wr_ref, mu_ref, wl_ref, bl_ref, g2_ref, kkw_ref, ka_ref, rk_ref,
     lnw_ref, lnb_ref, ones_ref) = refs[:12]
    refs = refs[12:]
    if has_vmix:
        v0_ref, v1_ref, v2_ref, vf_ref = refs[:4]
        refs = refs[4:]
        y_ref = refs[0]
        refs = refs[1:]
    else:
        y_ref, vf_out_ref = refs[:2]
        refs = refs[2:]
    (uprev, st, r_s, k_s, v_s, kk_s, b_s, lw_s, g_s, bg_s, y_s,
     ar_s, t_s, arb_s, bbar_s, av_s, arkv_s, vk_s, egc_s) = refs
    n_pairs = width // PACK
    step = pl.program_id(0)
    cur = step % 2
    prv = 1 - cur

    @pl.when(step == 0)
    def _():
        for ref in (g_s, bg_s, ar_s, t_s, arb_s, bbar_s, av_s, arkv_s, vk_s, egc_s):
            ref[1] = jnp.zeros(ref.shape[1:], ref.dtype)
        if not has_vmix:
            vf_out_ref[...] = jnp.zeros_like(vf_out_ref)

    @pl.when(step % tiles_per_seq == 0)
    def _():
        uprev[...] = jnp.zeros_like(uprev)

    @pl.when((step + tiles_per_seq - 1) % tiles_per_seq == 0)
    def _():
        st[...] = jnp.zeros_like(st)

    ones = ones_ref[...]

    def pre():
        u = _dot(h_ref[...], wr_ref[...])
        yield
        rolled = pltpu.roll(u, 1, 0)
        first = jnp.where(lax.broadcasted_iota(jnp.int32, (8, 1), 0) == 0, uprev[0:1, :], rolled[:8])
        u_prev = jnp.concatenate([first, rolled[8:]], axis=0)
        uprev[...] = pltpu.roll(u[ts - 8:ts, :], 1, 0)
        us = u + mu_ref[...] * (u_prev - u)
        r = us[:, 0:width]
        k = us[:, width:2 * width]
        v = us[:, 2 * width:3 * width]
        wa = us[:, 3 * width:3 * width + LANES]
        gl = us[:, 3 * width + LANES:3 * width + 2 * LANES]
        if has_vmix:
            z = _mm(_mm(v, v1_ref[...]), v2_ref[...])
            v = v + (vf_ref[...] - v) * _sigmoid(v0_ref[...] + z)
        else:
            vf_out_ref[...] = jnp.where(step < n_tiles, v, vf_out_ref[...])
        yield
        lane = lax.broadcasted_iota(jnp.int32, (1, LANES), 1)
        act = jnp.where(lane < HEAD, jnp.tanh(wa), wa)
        lo = _mm(act, wl_ref[...]) + bl_ref[...]
        lw_s[...] = DECAY_SCALE * _sigmoid(lo[:, :width])
        a = _sigmoid(lo[:, width:])
        yield
        g = _mm(_sigmoid(gl), g2_ref[...])
        kk = k * kkw_ref[...]
        ss = _segsum(kk * kk, ones)
        kk = kk * jnp.minimum(lax.rsqrt(ss), 1e12)
        yield
        kh = k * (1.0 + (a - 1.0) * ka_ref[...])
        bonus = _segsum(r * kh * rk_ref[...], ones) * v
        r_s[...] = r
        k_s[...] = kh
        v_s[...] = v
        kk_s[...] = kk
        b_s[...] = kk * a
        g_s[cur] = g
        bg_s[cur] = bonus * g
        yield

    masks = _pack_masks()
    tri = _tri_ones()

    lane_lo, blockdiag = masks[0], masks[5]
    n_groups = ts // (CHUNK * GROUP)
    inv_n = 1.0 / HEAD

    def store_to(c, p):
        def store(ar, t, arb, bbar, av, arkv, vk):
            ar_s[cur, c, p] = ar
            t_s[cur, c, p] = t
            arb_s[cur, c, p] = arb
            bbar_s[cur, c, p] = bbar
            av_s[cur, c, p] = av
            arkv_s[cur, c, p] = arkv
            vk_s[cur, c, p] = vk
        return store

    def phase1(gi):
        probs = []
        for j in range(GROUP):
            c = gi * GROUP + j
            sl = slice(c * CHUNK, (c + 1) * CHUNK)
            lwc = lw_s[sl, :]
            gc = _mm(tri, lwc, NN, 1, 3)
            e_g = jnp.exp(gc)
            e_ng = jnp.exp(-gc)
            e_gm = jnp.exp(gc - lwc)
            e_gc = jnp.exp(gc[CHUNK - 1:CHUNK, :])
            egc_s[cur, c] = jnp.broadcast_to(e_gc, (8, width))
            rt = r_s[sl, :] * e_g
            at = -kk_s[sl, :] * e_gm
            bh = b_s[sl, :] * e_ng
            kh_ = k_s[sl, :] * e_ng
            bbar = bh * e_gc
            kbar = kh_ * e_gc
            vc = v_s[sl, :]
            for p in range(n_pairs):
                ls = slice(p * PACK, (p + 1) * PACK)
                probs.append(dict(rt=rt[:, ls], at=at[:, ls], bh=bh[:, ls], kh=kh_[:, ls],
                                  bbar=bbar[:, ls], kbar=kbar[:, ls], v=vc[:, ls], store=store_to(c, p)))
        yield from _rwkv_phase1(probs, masks)

    def phase2(gi):
        pairs = range(n_pairs)
        for c in range(gi * GROUP, (gi + 1) * GROUP):
            sl = slice(c * CHUNK, (c + 1) * CHUNK)
            s = [st[p] for p in pairs]
            wy = [_dot(ar_s[prv, c, p], s[p].astype(BF16), NT) for p in pairs]
            yield
            u = [_dot(t_s[prv, c, p], _stack_heads(wy[p][:CHUNK] + av_s[prv, c, p], lane_lo)) for p in pairs]
            yield
            y = [wy[p][CHUNK:] + _dot(arb_s[prv, c, p], _stack_heads(u[p], lane_lo)) + arkv_s[prv, c, p]
                 for p in pairs]
            e_gc = egc_s[prv, c][0:1, :]
            for p in pairs:
                ls = slice(p * PACK, (p + 1) * PACK)
                upd = _dot(u[p].astype(BF16), bbar_s[prv, c, p], TN)
                st[p] = s[p] * e_gc[:, ls] + jnp.where(blockdiag, upd, 0.0) + vk_s[prv, c, p]
                y_s[sl, ls] = y[p]
            yield

    def post(gi):
        sl = slice(gi * GROUP * CHUNK, (gi + 1) * GROUP * CHUNK)
        y = y_s[sl, :]
        mean = _segsum(y, ones) * inv_n
        yield
        d = y - mean
        var = _segsum(d * d, ones) * inv_n
        yn = d * lax.rsqrt(var + RWKV_LN_EPS) * lnw_ref[...] + lnb_ref[...]
        y_ref[sl, :] = (yn * g_s[prv, sl, :] + bg_s[prv, sl, :]).astype(BF16)
        yield

    def stream_a():
        yield from pre()
        for gi in range(n_groups):
            yield from phase1(gi)

    def stream_b():
        for gi in range(n_groups):
            yield from phase2(gi)
            yield from post(gi)

    streams += [stream_a(), stream_b()]


def _lag_specs(n_tiles, ts):
    cur = lambda w: pl.BlockSpec((ts, w), lambda i: (jnp.minimum(i, n_tiles - 1), 0))
    lag = lambda w: pl.BlockSpec((ts, w), lambda i: (jnp.maximum(i - 1, 0), 0))
    return cur, lag


def _rwkv_parts(xf, p, vmix, v_first, *, seq, ts):
    t, d = xf.shape
    width = p["kkw"].shape[-1]
    has_vmix = vmix is not None
    n_tiles = t // ts
    cur, lag = _lag_specs(n_tiles, ts)
    names = ("wr", "mu", "wl", "bl", "g2", "kkw", "ka", "rk", "lnw", "lnb", "ones")
    args = [xf] + [_warg(p[n]) for n in names]
    in_specs = [cur(d)] + [_wspec(p[n]) for n in names]
    if has_vmix:
        args += [_warg(a) for a in vmix] + [v_first]
        in_specs += [_wspec(a) for a in vmix] + [cur(width)]
        out_shape = [jax.ShapeDtypeStruct((t, width), BF16)]
        out_specs = [lag(width)]
    else:
        out_shape = [jax.ShapeDtypeStruct((t, width), BF16), jax.ShapeDtypeStruct((t, width), F32)]
        out_specs = [lag(width), cur(width)]
    ucols = p["wr"].shape[1]
    tw = pltpu.VMEM((ts, width), F32)
    tw2 = pltpu.VMEM((2, ts, width), F32)
    nc, npairs = ts // CHUNK, width // PACK
    per = lambda rows, dt: pltpu.VMEM((2, nc, npairs, rows, PACK), dt)
    scratch = ([pltpu.VMEM((8, ucols), F32), pltpu.VMEM((npairs, PACK, PACK), F32)] + [tw] * 6 + [tw2, tw2, tw]
               + [per(2 * CHUNK, BF16), per(CHUNK, BF16), per(CHUNK, BF16), per(CHUNK, BF16),
                  per(CHUNK, F32), per(CHUNK, F32), per(PACK, F32), pltpu.VMEM((2, nc, 8, width), F32)])
    kw = dict(ts=ts, width=width, has_vmix=has_vmix, tiles_per_seq=seq // ts, n_tiles=n_tiles)
    return dict(args=args, in_specs=in_specs, out_shape=out_shape, out_specs=out_specs, scratch=scratch, kw=kw)


def _gla_kernel(h_ref, wg_ref, conv_ref, aup_ref, ab_ref, nw_ref, y_ref,
                carry, st, q_s, k_s, v_s, la_s, o_s, go_s, qg_s, kv_s, el_s,
                *, ts, kw, vw, n_heads, tiles_per_seq, streams):
    dv = vw // n_heads
    qkv_w = 2 * kw + vw
    step = pl.program_id(0)
    cur = step % 2
    prv = 1 - cur

    @pl.when(step == 0)
    def _():
        for ref in (o_s, go_s, qg_s, kv_s, el_s):
            ref[1] = jnp.zeros(ref.shape[1:], ref.dtype)

    @pl.when(step % tiles_per_seq == 0)
    def _():
        carry[...] = jnp.zeros_like(carry)

    @pl.when((step + tiles_per_seq - 1) % tiles_per_seq == 0)
    def _():
        st[...] = jnp.zeros_like(st)

    def pre_tile():
        u = _dot(h_ref[...], wg_ref[...])
        yield
        qkv = u[:, :qkv_w]
        go = u[:, qkv_w:qkv_w + vw]
        al = u[:, qkv_w + vw:qkv_w + vw + LANES]
        go_s[cur] = go * _sigmoid(go)
        rows = lax.broadcasted_iota(jnp.int32, (ts, 1), 0)
        prev8 = carry[...]
        taps = conv_ref.shape[0]
        out = qkv * conv_ref[taps - 1:taps, :]
        for j in range(1, taps):
            shifted = pltpu.roll(qkv, j, 0)
            fill = pltpu.roll(prev8, j, 0)
            first = jnp.where(rows[:8] < j, fill, shifted[:8])
            shifted = jnp.concatenate([first, shifted[8:]], axis=0)
            out = out + shifted * conv_ref[taps - 1 - j:taps - j, :]
            yield
        carry[...] = qkv[ts - 8:ts, :]
        qkv = out * _sigmoid(out)
        log_a = _mm(al, aup_ref[...]) + ab_ref[...]
        log_a = (jnp.minimum(log_a, 0.0) - jnp.log(1.0 + jnp.exp(-jnp.abs(log_a)))) * (1.0 / GLA_TAU)
        q_s[...] = qkv[:, :kw] * (float(kw // n_heads) ** -0.5)
        k_s[...] = qkv[:, kw:2 * kw]
        v_s[...] = qkv[:, 2 * kw:].astype(BF16)
        la_s[...] = log_a
        yield

    pair_lanes = _head_lanes(LANES)
    lane_lo = pair_lanes[0]
    r = lax.broadcasted_iota(jnp.int32, (CHUNK, CHUNK), 0)
    c = lax.broadcasted_iota(jnp.int32, (CHUNK, CHUNK), 1)
    lower = r >= c
    tri = lower.astype(BF16)
    n_groups = ts // (CHUNK * GLA_GROUP)
    n_kpairs = kw // LANES

    def phase1(gi):
        chunks = range(gi * GLA_GROUP, (gi + 1) * GLA_GROUP)
        pre = {}
        for ci in chunks:
            sl = slice(ci * CHUNK, (ci + 1) * CHUNK)
            gc = _mm(tri, la_s[sl, :], NN, 1, 3)
            e_g = jnp.exp(gc)
            e_ng = jnp.exp(-gc)
            e_last = jnp.exp(gc[CHUNK - 1:CHUNK, :])
            el_s[cur, ci] = jnp.broadcast_to(e_last, (8, kw))
            q = q_s[sl, :]
            k = k_s[sl, :]
            qg = q * e_g
            kg = k * e_ng
            qg_s[cur, ci] = qg.astype(BF16)
            pre[ci] = (qg, q * e_ng, kg.astype(BF16), (k * e_g).astype(BF16), (kg * e_last).astype(BF16))
        yield
        a_past = {(ci, pr): _dot(_stack_heads(pre[ci][0][:, pr * LANES:(pr + 1) * LANES], pair_lanes),
                                 pre[ci][2][:, pr * LANES:(pr + 1) * LANES], NT)
                  for ci in chunks for pr in range(n_kpairs)}
        yield
        a_fut = {(ci, pr): _dot(_stack_heads(pre[ci][1][:, pr * LANES:(pr + 1) * LANES], pair_lanes),
                                pre[ci][3][:, pr * LANES:(pr + 1) * LANES], NT)
                 for ci in chunks for pr in range(n_kpairs)}
        yield
        zero = jnp.zeros((CHUNK, LANES), BF16)
        for ci in chunks:
            sl = slice(ci * CHUNK, (ci + 1) * CHUNK)
            for hd in range(n_heads):
                pr, hh = divmod(hd, 2)
                hs = slice(hd * dv, (hd + 1) * dv)
                rs = slice(hh * CHUNK, (hh + 1) * CHUNK)
                a = jnp.where(lower, a_past[ci, pr][rs], a_fut[ci, pr][rs]).astype(BF16)
                vh = v_s[sl, hs]
                o_s[cur, sl, hs] = _dot(a, vh)
                kb = pre[ci][4][:, pr * LANES:(pr + 1) * LANES]
                kb = jnp.where(lane_lo, kb, zero) if hh == 0 else jnp.where(lane_lo, zero, kb)
                kv_s[cur, ci, hd] = _dot(vh, kb, TN)
            yield

    def phase2(gi):
        for ci in range(gi * GLA_GROUP, (gi + 1) * GLA_GROUP):
            sl = slice(ci * CHUNK, (ci + 1) * CHUNK)
            e_last = el_s[prv, ci][0:1, :]
            for hd in range(n_heads):
                pr = hd // 2
                ls = slice(pr * LANES, (pr + 1) * LANES)
                hs = slice(hd * dv, (hd + 1) * dv)
                s_h = st[hd]
                o_s[prv, sl, hs] += _dot(qg_s[prv, ci][:, ls], s_h.astype(BF16), NT)
                st[hd] = s_h * e_last[:, ls] + kv_s[prv, ci, hd]
            yield

    def post(gi):
        sl = slice(gi * GLA_GROUP * CHUNK, (gi + 1) * GLA_GROUP * CHUNK)
        for hd in range(n_heads):
            hs = slice(hd * dv, (hd + 1) * dv)
            o = o_s[prv, sl, hs]
            o = o * lax.rsqrt(jnp.mean(o * o, axis=-1, keepdims=True) + NORM_EPS) * nw_ref[:, hs]
            y_ref[sl, hs] = (o * go_s[prv, sl, hs]).astype(BF16)
            yield

    def stream_a():
        yield from pre_tile()
        for gi in range(n_groups):
            yield from phase1(gi)

    def stream_b():
        for gi in range(n_groups):
            yield from phase2(gi)
            yield from post(gi)

    streams += [stream_a(), stream_b()]


def _gla_parts(xf, p, *, seq, ts):
    t, d = xf.shape
    kw = p["aup"].shape[1]
    vw = p["nw"].shape[1]
    n_heads = kw // HEAD
    n_tiles = t // ts
    cur, lag = _lag_specs(n_tiles, ts)
    names = ("wg", "conv", "aup", "ab", "nw")
    tkw = pltpu.VMEM((ts, kw), F32)
    tvw2 = pltpu.VMEM((2, ts, vw), F32)
    nc = ts // CHUNK
    scratch = [pltpu.VMEM((8, 2 * kw + vw), F32), pltpu.VMEM((n_heads, vw // n_heads, LANES), F32),
               tkw, tkw, pltpu.VMEM((ts, vw), BF16), tkw, tvw2, tvw2,
               pltpu.VMEM((2, nc, CHUNK, kw), BF16),
               pltpu.VMEM((2, nc, n_heads, vw // n_heads, LANES), F32),
               pltpu.VMEM((2, nc, 8, kw), F32)]
    return dict(args=[xf] + [_warg(p[n]) for n in names], in_specs=[cur(d)] + [_wspec(p[n]) for n in names],
                out_shape=[jax.ShapeDtypeStruct((t, vw), BF16)], out_specs=[lag(vw)], scratch=scratch,
                kw=dict(ts=ts, kw=kw, vw=vw, n_heads=n_heads, tiles_per_seq=seq // ts))


def _mix_kernel(*refs, counts, rwkv_kw, gla_kw):
    it = iter(refs)
    r_in, g_in, r_out, g_out, r_scr, g_scr = [[next(it) for _ in range(n)] for n in counts]
    streams = []
    _rwkv_kernel(*r_in, *r_out, *r_scr, streams=streams, **rwkv_kw)
    _gla_kernel(*g_in, *g_out, *g_scr, streams=streams, **gla_kw)
    _interleave(*streams)


def _mix(hf, rp, vmix, v_first, gp, *, seq, ts=512):
    r = _rwkv_parts(hf, rp, vmix, v_first, seq=seq, ts=ts)
    g = _gla_parts(hf, gp, seq=seq, ts=ts)
    counts = tuple(len(x) for x in (r["args"], g["args"], r["out_shape"], g["out_shape"], r["scratch"], g["scratch"]))
    outs = pl.pallas_call(
        functools.partial(_mix_kernel, counts=counts, rwkv_kw=r["kw"], gla_kw=g["kw"]),
        grid=(hf.shape[0] // ts + 1,),
        in_specs=r["in_specs"] + g["in_specs"],
        out_specs=r["out_specs"] + g["out_specs"],
        out_shape=r["out_shape"] + g["out_shape"],
        scratch_shapes=r["scratch"] + g["scratch"],
        compiler_params=pltpu.CompilerParams(
            dimension_semantics=("arbitrary",), vmem_limit_bytes=VMEM_LIMIT),
        name="mix_vmix" if vmix is not None else "mix",
    )(*r["args"], *g["args"])
    return outs[:-1], outs[-1]


def _xa_kv_kernel(mem_ref, g_ref, wkv_ref, k_ref, v_ref, *, width):
    m = _rms(mem_ref[0], g_ref[...]).astype(BF16)
    kv = _dot(m, wkv_ref[...])
    k_ref[0] = kv[:, :width].astype(BF16)
    v_ref[0] = kv[:, width:].astype(BF16)


def _xa_kv(mem, g, wkv):
    b, m, d = mem.shape
    width = wkv.shape[1] // 2
    blk = lambda w: pl.BlockSpec((1, m, w), lambda i: (i, 0, 0))
    return pl.pallas_call(
        functools.partial(_xa_kv_kernel, width=width),
        grid=(b,),
        in_specs=[blk(d), _const_spec(g.shape), _wspec(wkv)],
        out_specs=(blk(width), blk(width)),
        out_shape=(jax.ShapeDtypeStruct((b, m, width), BF16),) * 2,
        compiler_params=pltpu.CompilerParams(dimension_semantics=("arbitrary",), vmem_limit_bytes=VMEM_LIMIT),
        name="xa_kv",
    )(mem, g, _warg(wkv))


def _xa_heads(h, wq_ref, k_ref, v_ref, hd):
    q = _dot(h, wq_ref[...]) * (float(hd) ** -0.5)
    outs = []
    for i in range(q.shape[1] // hd):
        hs = slice(i * hd, (i + 1) * hd)
        sc = _dot(q[:, hs].astype(BF16), k_ref[0, :, hs], NT)
        sc = sc - jnp.max(sc, axis=-1, keepdims=True)
        e = jnp.exp(sc)
        pr = e / jnp.sum(e, axis=-1, keepdims=True)
        outs.append(_dot(pr.astype(BF16), v_ref[0, :, hs]).astype(BF16))
    return jnp.concatenate(outs, axis=1)


def _merge_kernel(x_ref, h_ref, yr_ref, yg_ref, k_ref, v_ref, wq_ref, wgate_ref, wb_ref, wo_ref, o_ref, *, hd):
    x = x_ref[...]
    h = h_ref[...]
    d = x.shape[-1]
    yx = _xa_heads(h, wq_ref, k_ref, v_ref, hd)
    merged = None
    for j in range(3):
        gate = _sigmoid(_dot(h, wgate_ref[:, j * d:(j + 1) * d]))
        t = _dot((yr_ref[...], yg_ref[...], yx)[j], wb_ref[j]) * gate
        merged = t if merged is None else merged + t
    o_ref[...] = x + _dot(merged.astype(BF16), wo_ref[...])


def _merge(x, h, yr, yg, k, v, wq, wgate, wb, wo, *, seq, tm=1024, hd=128):
    t, d = x.shape
    bw = yr.shape[1]
    m, width = k.shape[1], k.shape[2]
    row = lambda w: pl.BlockSpec((tm, w), lambda i: (i, 0))
    kvs = pl.BlockSpec((1, m, width), lambda i: ((i * tm) // seq, 0, 0))
    return pl.pallas_call(
        functools.partial(_merge_kernel, hd=hd),
        grid=(t // tm,),
        in_specs=[row(d), row(d), row(bw), row(bw), kvs, kvs, _wspec(wq), _wspec(wgate), _wspec(wb), _wspec(wo)],
        out_specs=row(d),
        out_shape=jax.ShapeDtypeStruct((t, d), F32),
        compiler_params=pltpu.CompilerParams(dimension_semantics=("arbitrary",), vmem_limit_bytes=VMEM_LIMIT),
        name="merge",
    )(x, h, yr, yg, k, v, _warg(wq), _warg(wgate), _warg(wb), _warg(wo))


def _block_ones(width, block):
    i = jnp.arange(width) // block
    return (i[:, None] == i[None, :]).astype(BF16)


def kernel(x, mem, ffn1_norm, ffn1_w_in, ffn1_w_out, mix_norm, mem_norm, w_in, rwkv_mu, rwkv_w0, rwkv_w2, rwkv_a0, rwkv_a2, rwkv_g2, rwkv_k_k, rwkv_k_a, rwkv_r_k, rwkv_ln_w, rwkv_ln_b, rwkv_v0, rwkv_v1, rwkv_v2, gla_conv, gla_a_up, gla_a_bias, gla_norm, xa_w_kv, w_branch, w_out, ffn2_norm, ffn2_w_in, ffn2_w_out, final_norm):
    b, s, d = x.shape
    depth = w_in.shape[0]
    rw = rwkv_w0.shape[1]
    dl, al_ = rwkv_w2.shape[1], rwkv_a2.shape[1]
    gl_ = rwkv_g2.shape[1]
    kw = gla_a_up.shape[2]
    vw = gla_norm.shape[1]
    glora = gla_a_up.shape[1]
    xw = xa_w_kv.shape[2] // 2
    rcols = 3 * rw + dl + al_ + gl_
    gcols = 2 * kw + vw + glora + vw
    assert dl + al_ == LANES and gl_ == LANES and dl == HEAD

    row1 = lambda a: a[:, None, :]
    w_r = w_in[:, :, :rcols].astype(BF16)
    g0 = rcols
    w_gqkv = w_in[:, :, g0:g0 + 2 * kw + vw]
    w_gal = w_in[:, :, g0 + 2 * kw + vw:g0 + 2 * kw + vw + glora]
    w_ggo = w_in[:, :, g0 + 2 * kw + vw + glora:g0 + gcols]
    w_g = jnp.concatenate([w_gqkv, w_ggo, jnp.pad(w_gal, ((0, 0), (0, 0), (0, LANES - glora)))], axis=-1).astype(BF16)
    w_xq = w_in[:, :, g0 + gcols:g0 + gcols + xw].astype(BF16)
    w_gate = w_in[:, :, g0 + gcols + xw:].astype(BF16)
    zl = jnp.zeros((depth, dl, rw), F32)
    w_lora = jnp.concatenate([jnp.concatenate([rwkv_w2, zl], axis=-1),
                              jnp.concatenate([zl, rwkv_a2], axis=-1)], axis=1).astype(BF16)
    b_lora = row1(jnp.concatenate([rwkv_w0, rwkv_a0], axis=-1))
    mv = rwkv_v1.shape[2]
    v1p = jnp.pad(rwkv_v1, ((0, 0), (0, 0), (0, LANES - mv))).astype(BF16)
    v2p = jnp.pad(rwkv_v2, ((0, 0), (0, LANES - mv), (0, 0))).astype(BF16)
    aup = jnp.pad(gla_a_up, ((0, 0), (0, LANES - glora), (0, 0))).astype(BF16)
    ones = _block_ones(2 * LANES, HEAD)
    f1i, f1o = ffn1_w_in.astype(BF16), ffn1_w_out.astype(BF16)
    f2i, f2o = ffn2_w_in.astype(BF16), ffn2_w_out.astype(BF16)
    wkv = xa_w_kv.astype(BF16)
    wb = w_branch.astype(BF16)
    wo = w_out.astype(BF16)
    g2 = rwkv_g2.astype(BF16)

    xf = x.reshape(b * s, d)
    v_first = None
    for l in range(depth):
        xf, hf = _ffn(xf, ffn1_norm[l][None, :], _Layer(f1i, l), _Layer(f1o, l), mix_norm[l][None, :], "emit")
        rp = dict(wr=_Layer(w_r, l), mu=rwkv_mu[l][None, :], wl=_Layer(w_lora, l), bl=b_lora[l],
                  g2=_Layer(g2, l), kkw=rwkv_k_k[l][None, :], ka=rwkv_k_a[l][None, :],
                  rk=rwkv_r_k[l].reshape(1, rw), lnw=rwkv_ln_w[l][None, :], lnb=rwkv_ln_b[l][None, :], ones=ones)
        gp = dict(wg=_Layer(w_g, l), conv=gla_conv[l], aup=_Layer(aup, l),
                  ab=gla_a_bias[l][None, :], nw=gla_norm[l][None, :])
        if l == 0:
            (y_r, v_first), y_g = _mix(hf, rp, None, None, gp, seq=s)
        else:
            vmix = (rwkv_v0[l - 1][None, :], _Layer(v1p, l - 1), _Layer(v2p, l - 1))
            (y_r,), y_g = _mix(hf, rp, vmix, v_first, gp, seq=s)
        k_m, v_m = _xa_kv(mem, mem_norm[l][None, :], _Layer(wkv, l))
        xf = _merge(xf, hf, y_r, y_g, k_m, v_m, _Layer(w_xq, l), _Layer(w_gate, l), _Layer(wb, l), _Layer(wo, l),
                    seq=s)
        if l == depth - 1:
            xf = _ffn(xf, ffn2_norm[l][None, :], _Layer(f2i, l), _Layer(f2o, l), final_norm[None, :], "final")
        else:
            xf = _ffn(xf, ffn2_norm[l][None, :], _Layer(f2i, l), _Layer(f2o, l))
    return xf.reshape(b, s, d)
```
